```python
import math
import jax, jax.numpy as jnp
from jax import lax
import numpy as np

D_MODEL = 2048
BATCH = 16
SEQ = 256
DEPTH = 1
DEC_BATCH = 4
DEC_SEQ = 2048
PAST_LEN = 256

GRID_W = 64
HEAD_DIM = 128
HYENA_WIDTH = D_MODEL // 2
ATTN_WIDTH = D_MODEL - HYENA_WIDTH
N_Q_HEADS = ATTN_WIDTH // HEAD_DIM
N_KV_HEADS = 2
GQA_GROUP = N_Q_HEADS // N_KV_HEADS
QKV_DIM = (N_Q_HEADS + 2 * N_KV_HEADS) * HEAD_DIM
IN_DIM = 3 * HYENA_WIDTH + QKV_DIM
D_FF = 4 * D_MODEL
HYENA_ORDER = 2
SHORT_CONV = 3
FILTER_EMB = 33
FILTER_BANDS = (FILTER_EMB - 1) // 2
FILTER_HIDDEN = 64
DECAY_TARGET = 1e-2
DECAY_FAST_PCT = 0.3
DECAY_SLOW_PCT = 1.5
MIN_DECAY = math.log(DECAY_TARGET) / DECAY_SLOW_PCT
MAX_DECAY = math.log(DECAY_TARGET) / DECAY_FAST_PCT
ROPE_THETA = 10000.0
Q_BLOCK = 128
LN_EPS = 1e-5
RMS_EPS = 1e-6
DEEPNORM_ALPHA = (2 * DEPTH) ** 0.25
DEEPNORM_BETA = (8 * DEPTH) ** -0.25

kernel_name = 'hyena_gqa_hybrid_dit_step'

F32 = jnp.float32


def layer_norm(x, g, b):
    xf = x.astype(F32)
    mu = jnp.mean(xf, axis=-1, keepdims=True)
    var = jnp.mean(jnp.square(xf - mu), axis=-1, keepdims=True)
    y = (xf - mu) * lax.rsqrt(var + LN_EPS)
    return (y * g.astype(F32) + b.astype(F32)).astype(x.dtype)


def rms_norm(x, g):
    xf = x.astype(F32)
    y = xf * lax.rsqrt(jnp.mean(jnp.square(xf), axis=-1, keepdims=True) + RMS_EPS)
    return (y * g.astype(F32)).astype(x.dtype)


def centred_short_conv(u, w, b):
    L = u.shape[1]
    up = jnp.pad(u, ((0, 0), (1, 1), (0, 0)))
    return up[:, :L] * w[0] + up[:, 1:L + 1] * w[1] + up[:, 2:] * w[2] + b


def hyena_filter_spectra(L, w1, b1, f1, w2, b2, f2, w3, b3):
    t = jnp.linspace(0.0, 1.0, L, dtype=F32)[:, None]
    w = 2.0 * math.pi * jnp.arange(L, dtype=F32)[:, None] / L
    bands = jnp.linspace(1e-4, FILTER_BANDS - 1, FILTER_BANDS, dtype=F32)[None, :]
    z = jnp.concatenate([t, jnp.cos(bands * w), -jnp.sin(bands * w)], axis=-1)
    h = jnp.sin(f1.astype(F32) * (z @ w1.astype(F32) + b1.astype(F32)))
    h = jnp.sin(f2.astype(F32) * (h @ w2.astype(F32) + b2.astype(F32)))
    h = h @ w3.astype(F32) + b3.astype(F32)
    h = h.reshape(L, HYENA_ORDER, 2, HYENA_WIDTH)
    deltas = jnp.abs(jnp.linspace(MIN_DECAY, MAX_DECAY, HYENA_WIDTH, dtype=F32))
    decay = jnp.exp(-t * deltas[None, :])
    h = h * decay[:, None, None, :]
    h_fwd, h_bwd = h[:, :, 0], h[:, :, 1]
    circ = jnp.concatenate([h_fwd, jnp.zeros_like(h_fwd[:1]), jnp.flip(h_bwd[1:], axis=0)], axis=0)
    return jnp.fft.rfft(circ, axis=0)


def long_conv(z, kf, bias):
    L = z.shape[1]
    zf = z.astype(F32)
    y = jnp.fft.irfft(jnp.fft.rfft(zf, n=2 * L, axis=1) * kf[None], n=2 * L, axis=1)[:, :L]
    return (y + zf * bias.astype(F32)).astype(z.dtype)


def hyena_mixer(u, conv_w, conv_b, fw1, fb1, ff1, fw2, fb2, ff2, fw3, fb3, filt_bias):
    L = u.shape[1]
    u = centred_short_conv(u, conv_w, conv_b)
    v, x1, x2 = jnp.split(u, 3, axis=-1)
    kf = hyena_filter_spectra(L, fw1, fb1, ff1, fw2, fb2, ff2, fw3, fb3)
    z = x1 * long_conv(v, kf[:, 0], filt_bias[0])
    return x2 * long_conv(z, kf[:, 1], filt_bias[1])


def axial_rope(L):
    rows = L // GRID_W
    row = jnp.broadcast_to(jnp.arange(rows, dtype=F32)[:, None], (rows, GRID_W)).reshape(-1)
    col = jnp.broadcast_to(jnp.arange(GRID_W, dtype=F32)[None, :], (rows, GRID_W)).reshape(-1)
    n = HEAD_DIM // 4
    inv = ROPE_THETA ** (-jnp.arange(n, dtype=F32) / n)
    ang = jnp.concatenate([row[:, None] * inv, col[:, None] * inv], axis=-1)
    return jnp.cos(ang), jnp.sin(ang)


def apply_rope(x, cos, sin):
    half = HEAD_DIM // 2
    x1 = x[..., :half].astype(F32)
    x2 = x[..., half:].astype(F32)
    c = cos[None, :, None, :]
    s = sin[None, :, None, :]
    return jnp.concatenate([x1 * c - x2 * s, x1 * s + x2 * c], axis=-1).astype(x.dtype)


def block_attention(q, k, v):
    B, Lq = q.shape[0], q.shape[1]
    nblk = Lq // Q_BLOCK
    qb = jnp.moveaxis(q.reshape(B, nblk, Q_BLOCK, N_KV_HEADS, GQA_GROUP, HEAD_DIM), 1, 0)
    scale = HEAD_DIM ** -0.5

    def one_block(q_blk):
        s = jnp.einsum('bqkgd,bskd->bkgqs', q_blk, k).astype(F32) * scale
        p = jax.nn.softmax(s, axis=-1).astype(v.dtype)
        return jnp.einsum('bkgqs,bskd->bqkgd', p, v)

    o = lax.map(one_block, qb)
    return jnp.moveaxis(o, 0, 1).reshape(B, Lq, N_Q_HEADS * HEAD_DIM)


def trunk_layer(x, mod, ctx_kv, rope, lp):
    (w_in, conv_w, conv_b, fw1, fb1, ff1, fw2, fb2, ff2, fw3, fb3, filt_bias,
     q_gain, k_gain, w_out, ln1_g, ln1_b, w_up, w_down, ln2_g, ln2_b) = lp
    B, L = x.shape[0], x.shape[1]
    shift1, scale1, gate1, shift2, scale2, gate2 = jnp.split(mod, 6, axis=-1)
    h = x * (1 + scale1) + shift1
    proj = h @ w_in
    o0 = 3 * HYENA_WIDTH
    o1 = o0 + N_Q_HEADS * HEAD_DIM
    o2 = o1 + N_KV_HEADS * HEAD_DIM
    hy = hyena_mixer(proj[..., :o0], conv_w, conv_b, fw1, fb1, ff1, fw2, fb2, ff2, fw3, fb3, filt_bias)
    q = rms_norm(proj[..., o0:o1].reshape(B, L, N_Q_HEADS, HEAD_DIM), q_gain)
    k = rms_norm(proj[..., o1:o2].reshape(B, L, N_KV_HEADS, HEAD_DIM), k_gain)
    v = proj[..., o2:].reshape(B, L, N_KV_HEADS, HEAD_DIM)
    new_kv = (k, v)
    if rope is not None:
        q = apply_rope(q, rope[0], rope[1])
        k = apply_rope(k, rope[0], rope[1])
    if ctx_kv is not None:
        k_all = jnp.concatenate([ctx_kv[0], k], axis=1)
        v_all = jnp.concatenate([ctx_kv[1], v], axis=1)
    else:
        k_all, v_all = k, v
    attn = block_attention(q, k_all, v_all)
    mix = jnp.concatenate([hy, attn], axis=-1) @ w_out
    x = layer_norm(DEEPNORM_ALPHA * x + gate1 * mix, ln1_g, ln1_b)
    h = x * (1 + scale2) + shift2
    ff = jnp.square(jax.nn.relu(h @ w_up)) @ w_down
    x = layer_norm(DEEPNORM_ALPHA * x + gate2 * ff, ln2_g, ln2_b)
    return x, new_kv


def setup_inputs(seed: int = 0) -> dict:
    key = jax.random.key(seed)
    ks = jax.random.split(key, 32)
    nrm = jax.random.normal
    D, HY = D_MODEL, HYENA_WIDTH
    return {
        'x_prompt': nrm(ks[0], (BATCH, SEQ, D), F32),
        'x_sample': nrm(ks[1], (DEC_BATCH, DEC_SEQ, D), F32),
        'cache_k': nrm(ks[2], (DEC_BATCH, DEPTH, PAST_LEN, N_KV_HEADS, HEAD_DIM), F32),
        'cache_v': nrm(ks[3], (DEC_BATCH, DEPTH, PAST_LEN, N_KV_HEADS, HEAD_DIM), F32),
        'c': nrm(ks[4], (DEC_BATCH, D), F32),
        'c_ctx': nrm(ks[5], (D,), F32),
        'w_mod': nrm(ks[6], (DEPTH, D, 6 * D), F32) * (0.5 * D ** -0.5),
        'b_mod': nrm(ks[7], (DEPTH, 6 * D), F32) * 0.01,
        'w_in': nrm(ks[8], (DEPTH, D, IN_DIM), F32) * D ** -0.5,
        'conv_w': nrm(ks[9], (DEPTH, SHORT_CONV, 3 * HY), F32) * SHORT_CONV ** -0.5,
        'conv_b': nrm(ks[10], (DEPTH, 3 * HY), F32) * 0.01,
        'filt_w1': nrm(ks[11], (DEPTH, FILTER_EMB, FILTER_HIDDEN), F32) * FILTER_EMB ** -0.5,
        'filt_b1': nrm(ks[12], (DEPTH, FILTER_HIDDEN), F32) * 0.1,
        'filt_freq1': 1.0 + 0.1 * nrm(ks[13], (DEPTH, FILTER_HIDDEN), F32),
        'filt_w2': nrm(ks[14], (DEPTH, FILTER_HIDDEN, FILTER_HIDDEN), F32) * FILTER_HIDDEN ** -0.5,
        'filt_b2': nrm(ks[15], (DEPTH, FILTER_HIDDEN), F32) * 0.1,
        'filt_freq2': 1.0 + 0.1 * nrm(ks[16], (DEPTH, FILTER_HIDDEN), F32),
        'filt_w3': nrm(ks[17], (DEPTH, FILTER_HIDDEN, HYENA_ORDER * 2 * HY), F32) * 0.02,
        'filt_b3': nrm(ks[18], (DEPTH, HYENA_ORDER * 2 * HY), F32) * 0.01,
        'filt_bias': nrm(ks[19], (DEPTH, HYENA_ORDER, HY), F32),
        'q_gain': 1.0 + 0.01 * nrm(ks[20], (DEPTH, HEAD_DIM), F32),
        'k_gain': 1.0 + 0.01 * nrm(ks[21], (DEPTH, HEAD_DIM), F32),
        'w_out': nrm(ks[22], (DEPTH, D, D), F32) * (DEEPNORM_BETA * D ** -0.5),
        'ln1_g': 1.0 + 0.01 * nrm(ks[23], (DEPTH, D), F32),
        'ln1_b': 0.01 * nrm(ks[24], (DEPTH, D), F32),
        'w_up': nrm(ks[25], (DEPTH, D, D_FF), F32) * D ** -0.5,
        'w_down': nrm(ks[26], (DEPTH, D_FF, D), F32) * (DEEPNORM_BETA * D_FF ** -0.5),
        'ln2_g': 1.0 + 0.01 * nrm(ks[27], (DEPTH, D), F32),
        'ln2_b': 0.01 * nrm(ks[28], (DEPTH, D), F32),
    }


def reference(x_prompt, x_sample, cache_k, cache_v, c, c_ctx, w_mod, b_mod, w_in, conv_w, conv_b,
              filt_w1, filt_b1, filt_freq1, filt_w2, filt_b2, filt_freq2, filt_w3, filt_b3, filt_bias,
              q_gain, k_gain, w_out, ln1_g, ln1_b, w_up, w_down, ln2_g, ln2_b):
    rope = axial_rope(x_sample.shape[1])
    xp, xs = x_prompt, x_sample
    new_k, new_v = [], []
    for l in range(DEPTH):
        lp = (w_in[l], conv_w[l], conv_b[l], filt_w1[l], filt_b1[l], filt_freq1[l], filt_w2[l],
              filt_b2[l], filt_freq2[l], filt_w3[l], filt_b3[l], filt_bias[l], q_gain[l], k_gain[l],
              w_out[l], ln1_g[l], ln1_b[l], w_up[l], w_down[l], ln2_g[l], ln2_b[l])
        mod_ctx = (jax.nn.silu(c_ctx)[None] @ w_mod[l] + b_mod[l])[:, None, :]
        mod_lat = (jax.nn.silu(c) @ w_mod[l] + b_mod[l])[:, None, :]
        xp, kv = trunk_layer(xp, mod_ctx, None, None, lp)
        new_k.append(kv[0])
        new_v.append(kv[1])
        xs, _ = trunk_layer(xs, mod_lat, (cache_k[:, l], cache_v[:, l]), rope, lp)
    new_cache_k = jnp.stack(new_k, axis=1)
    new_cache_v = jnp.stack(new_v, axis=1)
    return (xp, xs, new_cache_k, new_cache_v)
```

```python
import functools
import math

import jax
import jax.numpy as jnp
from jax import lax
from jax.experimental import pallas as pl
from jax.experimental.pallas import tpu as pltpu

F32 = jnp.float32
BF16 = jnp.bfloat16

D_MODEL = 2048
GRID_W = 64
HEAD_DIM = 128
HYENA_WIDTH = D_MODEL // 2
N_Q_HEADS = (D_MODEL - HYENA_WIDTH) // HEAD_DIM
N_KV_HEADS = 2
GQA_GROUP = N_Q_HEADS // N_KV_HEADS
Q_DIM = N_Q_HEADS * HEAD_DIM
KV_DIM = N_KV_HEADS * HEAD_DIM
QKV_DIM = Q_DIM + 2 * KV_DIM
IN_DIM = 3 * HYENA_WIDTH + QKV_DIM
D_FF = 4 * D_MODEL
HYENA_ORDER = 2
FILTER_EMB = 33
FILTER_BANDS = (FILTER_EMB - 1) // 2
FILTER_HIDDEN = 64
MIN_DECAY = math.log(1e-2) / 1.5
MAX_DECAY = math.log(1e-2) / 0.3
ROPE_THETA = 10000.0
LN_EPS = 1e-5
RMS_EPS = 1e-6
DEPTH = 1
DEEPNORM_ALPHA = (2 * DEPTH) ** 0.25
N_MOD = 6
MOD_ROWS = 8

V7X_VMEM_LIMIT = 56 * 1024 * 1024
HIGHEST = lax.Precision.HIGHEST


def _params(semantics):
    return pltpu.CompilerParams(dimension_semantics=semantics, vmem_limit_bytes=V7X_VMEM_LIMIT)


def _layer_norm(y, g, b):
    mu = jnp.mean(y, axis=-1, keepdims=True)
    yc = y - mu
    var = jnp.mean(yc * yc, axis=-1, keepdims=True)
    return yc * lax.rsqrt(var + LN_EPS) * g + b


def _mod_kernel(c_ref, w_ref, b_ref, o_ref):
    c = c_ref[...]
    s = c / (1.0 + jnp.exp(-c))
    o_ref[...] = jnp.dot(s, w_ref[...], precision=HIGHEST, preferred_element_type=F32) + b_ref[...]


def _modulation(cc, w_mod, b_mod):
    tn = 1024
    n = N_MOD * D_MODEL
    return pl.pallas_call(
        _mod_kernel,
        grid=(n // tn,),
        in_specs=[
            pl.BlockSpec((MOD_ROWS, D_MODEL), lambda j: (0, 0)),
            pl.BlockSpec((D_MODEL, tn), lambda j: (0, j)),
            pl.BlockSpec((1, tn), lambda j: (0, j)),
        ],
        out_specs=pl.BlockSpec((MOD_ROWS, tn), lambda j: (0, j)),
        out_shape=jax.ShapeDtypeStruct((MOD_ROWS, n), F32),
        compiler_params=_params(("parallel",)),
        name="modulation",
    )(cc, w_mod, b_mod.reshape(1, n))


def _mod_spec(chunk, row0):
    return pl.BlockSpec((None, 1, D_MODEL), lambda b, *_: (b + row0, 0, chunk))


def _inproj_kernel(x_ref, sh_ref, sc_ref, w_ref, o_ref, h_scr):
    @pl.when(pl.program_id(2) == 0)
    def _():
        h_scr[...] = (x_ref[...] * (1.0 + sc_ref[...]) + sh_ref[...]).astype(BF16)

    o_ref[...] = jnp.dot(h_scr[...], w_ref[...], preferred_element_type=F32)


def _in_projection(x, mod3, row0, w_in):
    nb, length, _ = x.shape
    tm, tn = min(length, 1024), 512
    return pl.pallas_call(
        _inproj_kernel,
        grid=(nb, length // tm, IN_DIM // tn),
        in_specs=[
            pl.BlockSpec((None, tm, D_MODEL), lambda b, i, j: (b, i, 0)),
            _mod_spec(0, row0),
            _mod_spec(1, row0),
            pl.BlockSpec((D_MODEL, tn), lambda b, i, j: (0, j)),
        ],
        out_specs=pl.BlockSpec((None, tm, tn), lambda b, i, j: (b, i, j)),
        out_shape=jax.ShapeDtypeStruct((nb, length, IN_DIM), F32),
        scratch_shapes=[pltpu.VMEM((tm, D_MODEL), BF16)],
        compiler_params=_params(("parallel", "parallel", "arbitrary")),
        name="in_projection",
    )(x, mod3, mod3, w_in)


def _short_conv_kernel(pv_ref, p1_ref, p2_ref, wv_ref, w1_ref, w2_ref, bv_ref, b1_ref, b2_ref,
                       v_ref, x1_ref, x2_ref):
    def conv(p_ref, w_ref, b_ref, o_ref):
        p = p_ref[...]
        length = p.shape[0]
        row = lax.broadcasted_iota(jnp.int32, p.shape, 0)
        prev = jnp.where(row == 0, 0.0, pltpu.roll(p, 1, 0))
        nxt = jnp.where(row == length - 1, 0.0, pltpu.roll(p, length - 1, 0))
        o_ref[...] = (prev * w_ref[0:1, :] + p * w_ref[1:2, :] + nxt * w_ref[2:3, :] + b_ref[...]).astype(o_ref.dtype)

    conv(pv_ref, wv_ref, bv_ref, v_ref)
    conv(p1_ref, w1_ref, b1_ref, x1_ref)
    conv(p2_ref, w2_ref, b2_ref, x2_ref)


def _short_conv(proj, conv_w, conv_b):
    nb, length, _ = proj.shape
    tc = 256 if length > 512 else HYENA_WIDTH
    ncb = HYENA_WIDTH // tc
    p_specs = [pl.BlockSpec((None, length, tc), lambda b, j, g=g: (b, 0, g * ncb + j)) for g in range(3)]
    w_specs = [pl.BlockSpec((3, tc), lambda b, j, g=g: (0, g * ncb + j)) for g in range(3)]
    b_specs = [pl.BlockSpec((1, tc), lambda b, j, g=g: (0, g * ncb + j)) for g in range(3)]
    out = jax.ShapeDtypeStruct((nb, length, HYENA_WIDTH), BF16)
    return pl.pallas_call(
        _short_conv_kernel,
        grid=(nb, ncb),
        in_specs=p_specs + w_specs + b_specs,
        out_specs=[pl.BlockSpec((None, length, tc), lambda b, j: (b, 0, j))] * 3,
        out_shape=[out, out, out],
        compiler_params=_params(("parallel", "parallel")),
        name="short_conv",
    )(proj, proj, proj, conv_w, conv_w, conv_w, conv_b, conv_b, conv_b)


def _qkv_kernel(*refs, rope, emit_cache):
    p_ref, qg_ref, kg_ref = refs[:3]
    refs = refs[3:]
    if rope:
        cc_ref, ss_ref = refs[:2]
        refs = refs[2:]
    q_ref, k_ref, v_ref = refs[:3]
    if emit_cache:
        kc_ref, vc_ref = refs[3:]
    scale = HEAD_DIM ** -0.5
    for h in range(N_Q_HEADS + N_KV_HEADS):
        x = p_ref[:, h * HEAD_DIM:(h + 1) * HEAD_DIM]
        is_q = h < N_Q_HEADS
        gain = qg_ref[...] if is_q else kg_ref[...]
        y = x * lax.rsqrt(jnp.mean(x * x, axis=-1, keepdims=True) + RMS_EPS) * gain
        hk = h - N_Q_HEADS
        if emit_cache and not is_q:
            kc_ref[:, hk * HEAD_DIM:(hk + 1) * HEAD_DIM] = y
        if rope:
            y = y * cc_ref[...] + pltpu.roll(y, HEAD_DIM // 2, 1) * ss_ref[...]
        if is_q:
            q_ref[:, h * HEAD_DIM:(h + 1) * HEAD_DIM] = (y * scale).astype(BF16)
        else:
            k_ref[:, hk * HEAD_DIM:(hk + 1) * HEAD_DIM] = y.astype(BF16)
    v = p_ref[:, Q_DIM + KV_DIM:]
    v_ref[...] = v.astype(BF16)
    if emit_cache:
        vc_ref[...] = v


def _qkv_prep(proj, q_gain, k_gain, rope_tables, emit_cache):
    nb, length, _ = proj.shape
    tl = min(length, 512)
    rope = rope_tables is not None
    in_specs = [
        pl.BlockSpec((None, tl, QKV_DIM), lambda b, i: (b, i, 3 * HYENA_WIDTH // QKV_DIM)),
        pl.BlockSpec((1, HEAD_DIM), lambda b, i: (0, 0)),
        pl.BlockSpec((1, HEAD_DIM), lambda b, i: (0, 0)),
    ]
    args = [proj, q_gain.reshape(1, HEAD_DIM), k_gain.reshape(1, HEAD_DIM)]
    if rope:
        in_specs += [pl.BlockSpec((tl, HEAD_DIM), lambda b, i: (i, 0))] * 2
        args += list(rope_tables)
    widths = [(Q_DIM, BF16), (KV_DIM, BF16), (KV_DIM, BF16)]
    if emit_cache:
        widths += [(KV_DIM, F32), (KV_DIM, F32)]
    return pl.pallas_call(
        functools.partial(_qkv_kernel, rope=rope, emit_cache=emit_cache),
        grid=(nb, length // tl),
        in_specs=in_specs,
        out_specs=[pl.BlockSpec((None, tl, w), lambda b, i: (b, i, 0)) for w, _ in widths],
        out_shape=[jax.ShapeDtypeStruct((nb, length, w), dt) for w, dt in widths],
        compiler_params=_params(("parallel", "parallel")),
        name="qkv_prep",
    )(*args)


def _attention_kernel(q_ref, k_ref, v_ref, o_ref):
    tq = q_ref.shape[0]
    q = jnp.concatenate([q_ref[:, g * HEAD_DIM:(g + 1) * HEAD_DIM] for g in range(GQA_GROUP)], axis=0)
    s = lax.dot_general(q, k_ref[...], (((1,), (1,)), ((), ())), preferred_element_type=F32)
    p = jnp.exp(s - jnp.max(s, axis=-1, keepdims=True))
    denom = jnp.sum(p, axis=-1, keepdims=True)
    o = jnp.dot(p.astype(BF16), v_ref[...], preferred_element_type=F32) / denom
    for g in range(GQA_GROUP):
        o_ref[:, g * HEAD_DIM:(g + 1) * HEAD_DIM] = o[g * tq:(g + 1) * tq].astype(o_ref.dtype)


def _attention(q, k, v):
    nb, lq, _ = q.shape
    lk = k.shape[1]
    tq = 256
    gw = GQA_GROUP * HEAD_DIM
    return pl.pallas_call(
        _attention_kernel,
        grid=(nb, N_KV_HEADS, lq // tq),
        in_specs=[
            pl.BlockSpec((None, tq, gw), lambda b, h, i: (b, i, h)),
            pl.BlockSpec((None, lk, HEAD_DIM), lambda b, h, i: (b, 0, h)),
            pl.BlockSpec((None, lk, HEAD_DIM), lambda b, h, i: (b, 0, h)),
        ],
        out_specs=pl.BlockSpec((None, tq, gw), lambda b, h, i: (b, i, h)),
        out_shape=jax.ShapeDtypeStruct((nb, lq, Q_DIM), BF16),
        compiler_params=_params(("parallel", "parallel", "parallel")),
        name="attention",
    )(q, k, v)


def _dft_matrices(length):
    k = jnp.arange(length, dtype=jnp.int32)
    ang = ((k[:, None] * k[None, :]) % (2 * length)).astype(F32) * (math.pi / length)
    alt = (1 - 2 * (k % 2)).astype(F32)
    cos_m = jnp.cos(ang)
    sin_m = -jnp.sin(ang)
    sin_f = sin_m.at[0, :].set(alt)
    sin_b = sin_m.at[:, 0].set(alt)
    return cos_m.astype(BF16), sin_f.astype(BF16), sin_b.astype(BF16), alt.reshape(length, 1)


def _filter_kernel(z_ref, w1_ref, b1_ref, f1_ref, w2_ref, b2_ref, f2_ref, w3f_ref, w3b_ref, b3f_ref, b3b_ref,
                   t_ref, dl_ref, alt_ref, cos_ref, sin_ref, gr_ref, gi_ref, hs_scr, hd_scr, nyq_scr):
    length = z_ref.shape[0]
    f = pl.program_id(2)

    @pl.when(f == 0)
    def _():
        h = jnp.sin(f1_ref[...] * (jnp.dot(z_ref[...], w1_ref[...], precision=HIGHEST,
                                           preferred_element_type=F32) + b1_ref[...]))
        h = jnp.sin(f2_ref[...] * (jnp.dot(h, w2_ref[...], precision=HIGHEST,
                                           preferred_element_type=F32) + b2_ref[...]))
        decay = jnp.exp(-t_ref[...] * dl_ref[...])
        h_fwd = (jnp.dot(h, w3f_ref[...], precision=HIGHEST, preferred_element_type=F32) + b3f_ref[...]) * decay
        h_bwd = (jnp.dot(h, w3b_ref[...], precision=HIGHEST, preferred_element_type=F32) + b3b_ref[...]) * decay
        t_row = lax.broadcasted_iota(jnp.int32, h_fwd.shape, 0)
        h_bwd = jnp.where(t_row == 0, 0.0, h_bwd)
        h_sum = h_fwd + h_bwd
        hs_scr[...] = h_sum.astype(BF16)
        hd_scr[...] = (h_fwd - h_bwd).astype(BF16)
        nyq_scr[...] = jnp.sum(alt_ref[...] * h_sum, axis=0, keepdims=True)

    g_re = jnp.dot(cos_ref[...], hs_scr[...], preferred_element_type=F32)
    g_im = jnp.dot(sin_ref[...], hd_scr[...], preferred_element_type=F32)
    row = lax.broadcasted_iota(jnp.int32, g_re.shape, 0) + f * g_re.shape[0]
    g_im = jnp.where(row == 0, nyq_scr[...], g_im)
    norm = jnp.where(row == 0, 0.5 / length, 1.0 / length)
    gr_ref[...] = g_re * norm
    gi_ref[...] = g_im * norm


def _filter_spectra(length, dft, fw1, fb1, ff1, fw2, fb2, ff2, fw3, fb3):
    cos_m, sin_f, _, alt = dft
    t = jnp.linspace(0.0, 1.0, length, dtype=F32)[:, None]
    w = 2.0 * math.pi * jnp.arange(length, dtype=F32)[:, None] / length
    bands = jnp.linspace(1e-4, FILTER_BANDS - 1, FILTER_BANDS, dtype=F32)[None, :]
    pad = FILTER_HIDDEN - FILTER_EMB
    z = jnp.concatenate([t, jnp.cos(bands * w), -jnp.sin(bands * w), jnp.zeros((length, pad), F32)], axis=-1)
    w1 = jnp.concatenate([fw1, jnp.zeros((pad, FILTER_HIDDEN), F32)], axis=0)
    deltas = jnp.abs(jnp.linspace(MIN_DECAY, MAX_DECAY, HYENA_WIDTH, dtype=F32))[None, :]
    tc = 256
    tf = min(length, 512)
    ncb = HYENA_WIDTH // tc
    hid = FILTER_HIDDEN
    row = lambda a: a.reshape(1, -1)
    const = lambda shape: pl.BlockSpec(shape, lambda o, j, f: (0, 0))
    out = jax.ShapeDtypeStruct((HYENA_ORDER, length, HYENA_WIDTH), F32)
    return pl.pallas_call(
        _filter_kernel,
        grid=(HYENA_ORDER, ncb, length // tf),
        in_specs=[
            const((length, hid)), const((hid, hid)), const((1, hid)), const((1, hid)),
            const((hid, hid)), const((1, hid)), const((1, hid)),
            pl.BlockSpec((hid, tc), lambda o, j, f: (0, (2 * o) * ncb + j)),
            pl.BlockSpec((hid, tc), lambda o, j, f: (0, (2 * o + 1) * ncb + j)),
            pl.BlockSpec((1, tc), lambda o, j, f: (0, (2 * o) * ncb + j)),
            pl.BlockSpec((1, tc), lambda o, j, f: (0, (2 * o + 1) * ncb + j)),
            const((length, 1)),
            pl.BlockSpec((1, tc), lambda o, j, f: (0, j)),
            const((length, 1)),
            pl.BlockSpec((tf, length), lambda o, j, f: (f, 0)),
            pl.BlockSpec((tf, length), lambda o, j, f: (f, 0)),
        ],
        out_specs=[pl.BlockSpec((None, tf, tc), lambda o, j, f: (o, f, j))] * 2,
        out_shape=[out, out],
        scratch_shapes=[pltpu.VMEM((length, tc), BF16), pltpu.VMEM((length, tc), BF16), pltpu.VMEM((1, tc), F32)],
        compiler_params=_params(("parallel", "parallel", "arbitrary")),
        name="filter_spectra",
    )(z, w1, row(fb1), row(ff1), fw2, row(fb2), row(ff2), fw3, fw3, row(fb3), row(fb3), t, deltas, alt,
      cos_m, sin_f)


def _dft_fwd_kernel(z_ref, cos_ref, sin_ref, gr_ref, gi_ref, yr_ref, yi_ref):
    z = z_ref[...]
    z_re = jnp.dot(cos_ref[...], z, preferred_element_type=F32)
    z_im = jnp.dot(sin_ref[...], z, preferred_element_type=F32)
    g_re = gr_ref[...]
    g_im = gi_ref[...]
    rr = z_re * g_re
    ii = z_im * g_im
    tf = z_re.shape[0]
    row = lax.broadcasted_iota(jnp.int32, z_re.shape, 0) + pl.program_id(0) * tf
    packed = row == 0
    yr_ref[...] = jnp.where(packed, rr, rr - ii).astype(BF16)
    yi_ref[...] = jnp.where(packed, ii, z_re * g_im + z_im * g_re).astype(BF16)


def _dft_tiles(length):
    return (1024, 512) if length > 512 else (length, HYENA_WIDTH)


def _dft_forward(z, order, dft, g_re, g_im):
    nb, length, _ = z.shape
    cos_m, sin_f = dft[0], dft[1]
    tf, tc = _dft_tiles(length)
    ncb = HYENA_WIDTH // tc
    out = jax.ShapeDtypeStruct((nb, length, HYENA_WIDTH), BF16)
    return pl.pallas_call(
        _dft_fwd_kernel,
        grid=(length // tf, nb, ncb),
        in_specs=[
            pl.BlockSpec((None, length, tc), lambda f, b, j: (b, 0, j)),
            pl.BlockSpec((tf, length), lambda f, b, j: (f, 0)),
            pl.BlockSpec((tf, length), lambda f, b, j: (f, 0)),
            pl.BlockSpec((None, tf, tc), lambda f, b, j: (order, f, j)),
            pl.BlockSpec((None, tf, tc), lambda f, b, j: (order, f, j)),
        ],
        out_specs=[pl.BlockSpec((None, tf, tc), lambda f, b, j: (b, f, j))] * 2,
        out_shape=[out, out],
        compiler_params=_params(("parallel", "parallel", "parallel")),
        name="dft_forward",
    )(z, cos_m, sin_f, g_re, g_im)


def _dft_inv_kernel(yr_ref, yi_ref, cos_ref, sin_ref, zin_ref, mul_ref, bias_ref, o_ref):
    y = (jnp.dot(cos_ref[...], yr_ref[...], preferred_element_type=F32)
         + jnp.dot(sin_ref[...], yi_ref[...], preferred_element_type=F32))
    y = y + bias_ref[...] * zin_ref[...].astype(F32)
    o_ref[...] = (mul_ref[...].astype(F32) * y).astype(o_ref.dtype)


def _dft_inverse(y_re, y_im, order, dft, z_in, mul, filt_bias):
    nb, length, _ = z_in.shape
    cos_m, sin_b = dft[0], dft[2]
    tt, tc = _dft_tiles(length)
    ncb = HYENA_WIDTH // tc
    return pl.pallas_call(
        _dft_inv_kernel,
        grid=(length // tt, nb, ncb),
        in_specs=[
            pl.BlockSpec((None, length, tc), lambda t, b, j: (b, 0, j)),
            pl.BlockSpec((None, length, tc), lambda t, b, j: (b, 0, j)),
            pl.BlockSpec((tt, length), lambda t, b, j: (t, 0)),
            pl.BlockSpec((tt, length), lambda t, b, j: (t, 0)),
            pl.BlockSpec((None, tt, tc), lambda t, b, j: (b, t, j)),
            pl.BlockSpec((None, tt, tc), lambda t, b, j: (b, t, j)),
            pl.BlockSpec((None, 1, tc), lambda t, b, j: (order, 0, j)),
        ],
        out_specs=pl.BlockSpec((None, tt, tc), lambda t, b, j: (b, t, j)),
        out_shape=jax.ShapeDtypeStruct((nb, length, HYENA_WIDTH), BF16),
        compiler_params=_params(("parallel", "parallel", "parallel")),
        name="dft_inverse",
    )(y_re, y_im, cos_m, sin_b, z_in, mul, filt_bias)


def _hyena_long_convs(v, x1, x2, dft, g_re, g_im, filt_bias):
    y_re, y_im = _dft_forward(v, 0, dft, g_re, g_im)
    z = _dft_inverse(y_re, y_im, 0, dft, v, x1, filt_bias)
    y_re, y_im = _dft_forward(z, 1, dft, g_re, g_im)
    return _dft_inverse(y_re, y_im, 1, dft, z, x2, filt_bias)


def _outproj_kernel(hy_ref, at_ref, x_ref, gate_ref, wa_ref, wb_ref, g_ref, b_ref, o_ref):
    mix = (jnp.dot(hy_ref[...], wa_ref[...], preferred_element_type=F32)
           + jnp.dot(at_ref[...], wb_ref[...], preferred_element_type=F32))
    y = DEEPNORM_ALPHA * x_ref[...] + gate_ref[...] * mix
    o_ref[...] = _layer_norm(y, g_ref[...], b_ref[...])


def _out_projection(hy, attn, x, mod3, row0, w_out, ln_g, ln_b):
    nb, length, _ = x.shape
    tm = min(length, 512)
    half = D_MODEL // 2
    return pl.pallas_call(
        _outproj_kernel,
        grid=(nb, length // tm),
        in_specs=[
            pl.BlockSpec((None, tm, half), lambda b, i: (b, i, 0)),
            pl.BlockSpec((None, tm, half), lambda b, i: (b, i, 0)),
            pl.BlockSpec((None, tm, D_MODEL), lambda b, i: (b, i, 0)),
            _mod_spec(2, row0),
            pl.BlockSpec((half, D_MODEL), lambda b, i: (0, 0)),
            pl.BlockSpec((half, D_MODEL), lambda b, i: (1, 0)),
            pl.BlockSpec((1, D_MODEL), lambda b, i: (0, 0)),
            pl.BlockSpec((1, D_MODEL), lambda b, i: (0, 0)),
        ],
        out_specs=pl.BlockSpec((None, tm, D_MODEL), lambda b, i: (b, i, 0)),
        out_shape=jax.ShapeDtypeStruct((nb, length, D_MODEL), F32),
        compiler_params=_params(("parallel", "parallel")),
        name="out_projection",
    )(hy, attn, x, mod3, w_out, w_out, ln_g.reshape(1, D_MODEL), ln_b.reshape(1, D_MODEL))


def _mlp_kernel(x_ref, sh_ref, sc_ref, gate_ref, wu_ref, wd_ref, g_ref, b_ref, o_ref, h_scr, acc_scr):
    f = pl.program_id(2)

    @pl.when(f == 0)
    def _():
        h_scr[...] = (x_ref[...] * (1.0 + sc_ref[...]) + sh_ref[...]).astype(BF16)

    a = jnp.maximum(jnp.dot(h_scr[...], wu_ref[...], preferred_element_type=F32), 0.0)
    part = jnp.dot((a * a).astype(BF16), wd_ref[...], preferred_element_type=F32)

    @pl.when(f == 0)
    def _():
        acc_scr[...] = part

    @pl.when(f > 0)
    def _():
        acc_scr[...] += part

    @pl.when(f == pl.num_programs(2) - 1)
    def _():
        y = DEEPNORM_ALPHA * x_ref[...] + gate_ref[...] * acc_scr[...]
        o_ref[...] = _layer_norm(y, g_ref[...], b_ref[...])


def _mlp(x, mod3, row0, w_up, w_down, ln_g, ln_b):
    nb, length, _ = x.shape
    tm, tf = min(length, 512), 1024
    return pl.pallas_call(
        _mlp_kernel,
        grid=(nb, length // tm, D_FF // tf),
        in_specs=[
            pl.BlockSpec((None, tm, D_MODEL), lambda b, i, f: (b, i, 0)),
            _mod_spec(3, row0),
            _mod_spec(4, row0),
            _mod_spec(5, row0),
            pl.BlockSpec((D_MODEL, tf), lambda b, i, f: (0, f)),
            pl.BlockSpec((tf, D_MODEL), lambda b, i, f: (f, 0)),
            pl.BlockSpec((1, D_MODEL), lambda b, i, f: (0, 0)),
            pl.BlockSpec((1, D_MODEL), lambda b, i, f: (0, 0)),
        ],
        out_specs=pl.BlockSpec((None, tm, D_MODEL), lambda b, i, f: (b, i, 0)),
        out_shape=jax.ShapeDtypeStruct((nb, length, D_MODEL), F32),
        scratch_shapes=[pltpu.VMEM((tm, D_MODEL), BF16), pltpu.VMEM((tm, D_MODEL), F32)],
        compiler_params=_params(("parallel", "parallel", "arbitrary")),
        name="mlp",
    )(x, mod3, mod3, mod3, w_up, w_down, ln_g.reshape(1, D_MODEL), ln_b.reshape(1, D_MODEL))


def _rope_tables(length):
    rows = length // GRID_W
    row = jnp.broadcast_to(jnp.arange(rows, dtype=F32)[:, None], (rows, GRID_W)).reshape(-1)
    col = jnp.broadcast_to(jnp.arange(GRID_W, dtype=F32)[None, :], (rows, GRID_W)).reshape(-1)
    n = HEAD_DIM // 4
    inv = ROPE_THETA ** (-jnp.arange(n, dtype=F32) / n)
    ang = jnp.concatenate([row[:, None] * inv, col[:, None] * inv], axis=-1)
    cos, sin = jnp.cos(ang), jnp.sin(ang)
    return jnp.concatenate([cos, cos], axis=-1), jnp.concatenate([-sin, sin], axis=-1)


def _trunk_layer(x, mod3, row0, ctx_kv, rope_tables, emit_cache, weights, filt):
    (w_in, conv_w, conv_b, filt_bias, q_gain, k_gain, w_out, ln1_g, ln1_b, w_up, w_down, ln2_g, ln2_b) = weights
    dft, g_re, g_im = filt
    proj = _in_projection(x, mod3, row0, w_in)
    v, x1, x2 = _short_conv(proj, conv_w, conv_b)
    hy = _hyena_long_convs(v, x1, x2, dft, g_re, g_im, filt_bias)
    qkv = _qkv_prep(proj, q_gain, k_gain, rope_tables, emit_cache)
    q, k, vv = qkv[:3]
    if ctx_kv is not None:
        k = jnp.concatenate([ctx_kv[0], k], axis=1)
        vv = jnp.concatenate([ctx_kv[1], vv], axis=1)
    attn = _attention(q, k, vv)
    x = _out_projection(hy, attn, x, mod3, row0, w_out, ln1_g, ln1_b)
    x = _mlp(x, mod3, row0, w_up, w_down, ln2_g, ln2_b)
    return x, qkv[3:]


def kernel(x_prompt, x_sample, cache_k, cache_v, c, c_ctx, w_mod, b_mod, w_in, conv_w, conv_b, filt_w1, filt_b1, filt_freq1, filt_w2, filt_b2, filt_freq2, filt_w3, filt_b3, filt_bias, q_gain, k_gain, w_out, ln1_g, ln1_b, w_up, w_down, ln2_g, ln2_b):
    batch, seq, _ = x_prompt.shape
    dec_batch, dec_seq, _ = x_sample.shape
    past = cache_k.shape[2]
    assert w_mod.shape[0] == DEPTH == 1 and 1 + dec_batch <= MOD_ROWS

    cc = jnp.concatenate([c_ctx[None], c, jnp.zeros((MOD_ROWS - 1 - dec_batch, D_MODEL), F32)], axis=0)
    mod3 = _modulation(cc, w_mod[0], b_mod[0]).reshape(MOD_ROWS, 1, N_MOD * D_MODEL)

    weights = (w_in[0].astype(BF16), conv_w[0], conv_b[0].reshape(1, -1),
               filt_bias[0].reshape(HYENA_ORDER, 1, HYENA_WIDTH), q_gain[0], k_gain[0],
               w_out[0].astype(BF16), ln1_g[0], ln1_b[0], w_up[0].astype(BF16), w_down[0].astype(BF16),
               ln2_g[0], ln2_b[0])
    filt_w = (filt_w1[0], filt_b1[0], filt_freq1[0], filt_w2[0], filt_b2[0], filt_freq2[0], filt_w3[0], filt_b3[0])

    def filt(length):
        dft = _dft_matrices(length)
        g_re, g_im = _filter_spectra(length, dft, *filt_w)
        return dft, g_re, g_im

    per_seq = lambda a: a.reshape(batch, seq, a.shape[-1])
    flat = lambda a: a.reshape(1, batch * seq, a.shape[-1])
    (w_in_b, conv_w0, conv_b0, fbias, qg, kg, w_out_b, l1g, l1b, w_up_b, w_down_b, l2g, l2b) = weights
    dft_p, gre_p, gim_p = filt(seq)
    xp = flat(x_prompt)
    proj = _in_projection(xp, mod3, 0, w_in_b)
    v, x1, x2 = _short_conv(per_seq(proj), conv_w0, conv_b0)
    hy = _hyena_long_convs(v, x1, x2, dft_p, gre_p, gim_p, fbias)
    q, k, vv, k_cache, v_cache = _qkv_prep(proj, qg, kg, None, True)
    attn = _attention(per_seq(q), per_seq(k), per_seq(vv))
    xp = _out_projection(flat(hy), flat(attn), xp, mod3, 0, w_out_b, l1g, l1b)
    xp = _mlp(xp, mod3, 0, w_up_b, w_down_b, l2g, l2b)
    y_prompt = xp.reshape(batch, seq, D_MODEL)
    new_cache_k = k_cache.reshape(batch, 1, seq, N_KV_HEADS, HEAD_DIM)
    new_cache_v = v_cache.reshape(batch, 1, seq, N_KV_HEADS, HEAD_DIM)

    ctx_kv = (cache_k[:, 0].reshape(dec_batch, past, KV_DIM).astype(BF16),
              cache_v[:, 0].reshape(dec_batch, past, KV_DIM).astype(BF16))
    y_sample, _ = _trunk_layer(x_sample, mod3, 1, ctx_kv, _rope_tables(dec_seq), False, weights, filt(dec_seq))
    return (y_prompt, y_sample, new_cache_k, new_cache_v)
```

```python
import functools
import math

import jax
import jax.numpy as jnp
from jax import lax
from jax.experimental import pallas as pl
from jax.experimental.pallas import tpu as pltpu

F32 = jnp.float32
BF16 = jnp.bfloat16

D_MODEL = 2048
GRID_W = 64
HEAD_DIM = 128
HYENA_WIDTH = D_MODEL // 2
N_Q_HEADS = (D_MODEL - HYENA_WIDTH) // HEAD_DIM
N_KV_HEADS = 2
GQA_GROUP = N_Q_HEADS // N_KV_HEADS
Q_DIM = N_Q_HEADS * HEAD_DIM
KV_DIM = N_KV_HEADS * HEAD_DIM
QKV_DIM = Q_DIM + 2 * KV_DIM
IN_DIM = 3 * HYENA_WIDTH + QKV_DIM
D_FF = 4 * D_MODEL
HYENA_ORDER = 2
FILTER_EMB = 33
FILTER_BANDS = (FILTER_EMB - 1) // 2
FILTER_HIDDEN = 64
MIN_DECAY = math.log(1e-2) / 1.5
MAX_DECAY = math.log(1e-2) / 0.3
ROPE_THETA = 10000.0
LN_EPS = 1e-5
RMS_EPS = 1e-6
DEPTH = 1
DEEPNORM_ALPHA = (2 * DEPTH) ** 0.25
N_MOD = 6
MOD_ROWS = 8

V7X_VMEM_LIMIT = 56 * 1024 * 1024
HIGHEST = lax.Precision.HIGHEST


def _params(semantics):
    return pltpu.CompilerParams(dimension_semantics=semantics, vmem_limit_bytes=V7X_VMEM_LIMIT)


def _layer_norm(y, g, b):
    mu = jnp.mean(y, axis=-1, keepdims=True)
    yc = y - mu
    var = jnp.mean(yc * yc, axis=-1, keepdims=True)
    return yc * lax.rsqrt(var + LN_EPS) * g + b


def _mod_kernel(c_ref, w_ref, b_ref, o_ref):
    c = c_ref[...]
    s = c / (1.0 + jnp.exp(-c))
    o_ref[...] = jnp.dot(s, w_ref[...], precision=HIGHEST, preferred_element_type=F32) + b_ref[...]


def _modulation(cc, w_mod, b_mod):
    tn = 1024
    n = N_MOD * D_MODEL
    return pl.pallas_call(
        _mod_kernel,
        grid=(n // tn,),
        in_specs=[
            pl.BlockSpec((MOD_ROWS, D_MODEL), lambda j: (0, 0)),
            pl.BlockSpec((D_MODEL, tn), lambda j: (0, j)),
            pl.BlockSpec((1, tn), lambda j: (0, j)),
        ],
        out_specs=pl.BlockSpec((MOD_ROWS, tn), lambda j: (0, j)),
        out_shape=jax.ShapeDtypeStruct((MOD_ROWS, n), F32),
        compiler_params=_params(("parallel",)),
        name="modulation",
    )(cc, w_mod, b_mod.reshape(1, n))


def _mod_spec(chunk, row0):
    return pl.BlockSpec((None, 1, D_MODEL), lambda b, *_: (b + row0, 0, chunk))


def _inproj_kernel(x_ref, sh_ref, sc_ref, w_ref, o_ref, h_scr):
    @pl.when(pl.program_id(2) == 0)
    def _():
        h_scr[...] = (x_ref[...] * (1.0 + sc_ref[...]) + sh_ref[...]).astype(BF16)

    o_ref[...] = jnp.dot(h_scr[...], w_ref[...], preferred_element_type=F32)


def _in_projection(x, mod3, row0, w_in):
    nb, length, _ = x.shape
    tm, tn = min(length, 1024), IN_DIM // 3
    return pl.pallas_call(
        _inproj_kernel,
        grid=(nb, length // tm, IN_DIM // tn),
        in_specs=[
            pl.BlockSpec((None, tm, D_MODEL), lambda b, i, j: (b, i, 0)),
            _mod_spec(0, row0),
            _mod_spec(1, row0),
            pl.BlockSpec((D_MODEL, tn), lambda b, i, j: (0, j)),
        ],
        out_specs=pl.BlockSpec((None, tm, tn), lambda b, i, j: (b, i, j)),
        out_shape=jax.ShapeDtypeStruct((nb, length, IN_DIM), F32),
        scratch_shapes=[pltpu.VMEM((tm, D_MODEL), BF16)],
        compiler_params=_params(("parallel", "parallel", "arbitrary")),
        name="in_projection",
    )(x, mod3, mod3, w_in)


def _short_conv_kernel(pv_ref, p1_ref, p2_ref, wv_ref, w1_ref, w2_ref, bv_ref, b1_ref, b2_ref,
                       v_ref, x1_ref, x2_ref):
    def conv(p_ref, w_ref, b_ref, o_ref):
        p = p_ref[...]
        length = p.shape[0]
        row = lax.broadcasted_iota(jnp.int32, p.shape, 0)
        prev = jnp.where(row == 0, 0.0, pltpu.roll(p, 1, 0))
        nxt = jnp.where(row == length - 1, 0.0, pltpu.roll(p, length - 1, 0))
        o_ref[...] = (prev * w_ref[0:1, :] + p * w_ref[1:2, :] + nxt * w_ref[2:3, :] + b_ref[...]).astype(o_ref.dtype)

    conv(pv_ref, wv_ref, bv_ref, v_ref)
    conv(p1_ref, w1_ref, b1_ref, x1_ref)
    conv(p2_ref, w2_ref, b2_ref, x2_ref)


def _short_conv(proj, conv_w, conv_b):
    nb, length, _ = proj.shape
    tc = 256 if length > 512 else HYENA_WIDTH
    ncb = HYENA_WIDTH // tc
    p_specs = [pl.BlockSpec((None, length, tc), lambda b, j, g=g: (b, 0, g * ncb + j)) for g in range(3)]
    w_specs = [pl.BlockSpec((3, tc), lambda b, j, g=g: (0, g * ncb + j)) for g in range(3)]
    b_specs = [pl.BlockSpec((1, tc), lambda b, j, g=g: (0, g * ncb + j)) for g in range(3)]
    out = jax.ShapeDtypeStruct((nb, length, HYENA_WIDTH), BF16)
    return pl.pallas_call(
        _short_conv_kernel,
        grid=(nb, ncb),
        in_specs=p_specs + w_specs + b_specs,
        out_specs=[pl.BlockSpec((None, length, tc), lambda b, j: (b, 0, j))] * 3,
        out_shape=[out, out, out],
        compiler_params=_params(("parallel", "parallel")),
        name="short_conv",
    )(proj, proj, proj, conv_w, conv_w, conv_w, conv_b, conv_b, conv_b)


def _qkv_kernel(*refs, rope, emit_cache):
    p_ref, qg_ref, kg_ref = refs[:3]
    refs = refs[3:]
    if rope:
        cc_ref, ss_ref = refs[:2]
        refs = refs[2:]
    q_ref, k_ref, v_ref = refs[:3]
    if emit_cache:
        kc_ref, vc_ref = refs[3:]
    scale = HEAD_DIM ** -0.5
    for h in range(N_Q_HEADS + N_KV_HEADS):
        x = p_ref[:, h * HEAD_DIM:(h + 1) * HEAD_DIM]
        is_q = h < N_Q_HEADS
        gain = qg_ref[...] if is_q else kg_ref[...]
        y = x * lax.rsqrt(jnp.mean(x * x, axis=-1, keepdims=True) + RMS_EPS) * gain
        hk = h - N_Q_HEADS
        if emit_cache and not is_q:
            kc_ref[:, hk * HEAD_DIM:(hk + 1) * HEAD_DIM] = y
        if rope:
            y = y * cc_ref[...] + pltpu.roll(y, HEAD_DIM // 2, 1) * ss_ref[...]
        if is_q:
            q_ref[:, h * HEAD_DIM:(h + 1) * HEAD_DIM] = (y * scale).astype(BF16)
        else:
            k_ref[:, hk * HEAD_DIM:(hk + 1) * HEAD_DIM] = y.astype(BF16)
    v = p_ref[:, Q_DIM + KV_DIM:]
    v_ref[...] = v.astype(BF16)
    if emit_cache:
        vc_ref[...] = v


def _qkv_prep(proj, q_gain, k_gain, rope_tables, emit_cache):
    nb, length, _ = proj.shape
    tl = min(length, 512)
    rope = rope_tables is not None
    in_specs = [
        pl.BlockSpec((None, tl, QKV_DIM), lambda b, i: (b, i, 3 * HYENA_WIDTH // QKV_DIM)),
        pl.BlockSpec((1, HEAD_DIM), lambda b, i: (0, 0)),
        pl.BlockSpec((1, HEAD_DIM), lambda b, i: (0, 0)),
    ]
    args = [proj, q_gain.reshape(1, HEAD_DIM), k_gain.reshape(1, HEAD_DIM)]
    if rope:
        in_specs += [pl.BlockSpec((tl, HEAD_DIM), lambda b, i: (i, 0))] * 2
        args += list(rope_tables)
    widths = [(Q_DIM, BF16), (KV_DIM, BF16), (KV_DIM, BF16)]
    if emit_cache:
        widths += [(KV_DIM, F32), (KV_DIM, F32)]
    return pl.pallas_call(
        functools.partial(_qkv_kernel, rope=rope, emit_cache=emit_cache),
        grid=(nb, length // tl),
        in_specs=in_specs,
        out_specs=[pl.BlockSpec((None, tl, w), lambda b, i: (b, i, 0)) for w, _ in widths],
        out_shape=[jax.ShapeDtypeStruct((nb, length, w), dt) for w, dt in widths],
        compiler_params=_params(("parallel", "parallel")),
        name="qkv_prep",
    )(*args)


def _attention_kernel(q_ref, k_ref, v_ref, o_ref):
    tq = q_ref.shape[0]
    q = jnp.concatenate([q_ref[:, g * HEAD_DIM:(g + 1) * HEAD_DIM] for g in range(GQA_GROUP)], axis=0)
    st = lax.dot_general(k_ref[...], q, (((1,), (1,)), ((), ())), preferred_element_type=F32)
    pt = jnp.exp(st - jnp.max(st, axis=0, keepdims=True))
    denom = jnp.sum(pt, axis=0, keepdims=True)
    ot = lax.dot_general(v_ref[...], pt.astype(BF16), (((0,), (0,)), ((), ())), preferred_element_type=F32)
    o = (ot / denom).T
    for g in range(GQA_GROUP):
        o_ref[:, g * HEAD_DIM:(g + 1) * HEAD_DIM] = o[g * tq:(g + 1) * tq].astype(o_ref.dtype)


def _attention(q, k, v):
    nb, lq, _ = q.shape
    lk = k.shape[1]
    tq = 256
    gw = GQA_GROUP * HEAD_DIM
    return pl.pallas_call(
        _attention_kernel,
        grid=(nb, N_KV_HEADS, lq // tq),
        in_specs=[
            pl.BlockSpec((None, tq, gw), lambda b, h, i: (b, i, h)),
            pl.BlockSpec((None, lk, HEAD_DIM), lambda b, h, i: (b, 0, h)),
            pl.BlockSpec((None, lk, HEAD_DIM), lambda b, h, i: (b, 0, h)),
        ],
        out_specs=pl.BlockSpec((None, tq, gw), lambda b, h, i: (b, i, h)),
        out_shape=jax.ShapeDtypeStruct((nb, lq, Q_DIM), BF16),
        compiler_params=_params(("parallel", "parallel", "parallel")),
        name="attention",
    )(q, k, v)


DFT_ROW_BLOCK = 64


def _dft_table_kernel(c1_ref, s1_ref, c2_ref, s2_ref, cos_ref, sinf_ref, sinb_ref):
    c1, s1, c2, s2 = c1_ref[...], s1_ref[...], c2_ref[...], s2_ref[...]
    cos = c1 * c2 - s1 * s2
    sin = -(s1 * c2 + c1 * s2)
    row = lax.broadcasted_iota(jnp.int32, cos.shape, 0) + pl.program_id(0) * cos.shape[0]
    col = lax.broadcasted_iota(jnp.int32, cos.shape, 1)
    cos_ref[...] = cos.astype(BF16)
    sinf_ref[...] = jnp.where(row == 0, (1 - 2 * (col & 1)).astype(F32), sin).astype(BF16)
    sinb_ref[...] = jnp.where(col == 0, (1 - 2 * (row & 1)).astype(F32), sin).astype(BF16)


def _dft_matrices(length):
    rb = DFT_ROW_BLOCK
    t = jnp.arange(length, dtype=jnp.int32)[None, :]

    def table(mult):
        ang = ((mult * t) % (2 * length)).astype(F32) * (math.pi / length)
        return jnp.cos(ang), jnp.sin(ang)

    c1, s1 = table(rb * jnp.arange(length // rb, dtype=jnp.int32)[:, None])
    c2, s2 = table(jnp.arange(rb, dtype=jnp.int32)[:, None])
    coarse = pl.BlockSpec((None, 1, length), lambda a: (a, 0, 0))
    fine = pl.BlockSpec((rb, length), lambda a: (0, 0))
    out = jax.ShapeDtypeStruct((length, length), BF16)
    cos_m, sin_f, sin_b = pl.pallas_call(
        _dft_table_kernel,
        grid=(length // rb,),
        in_specs=[coarse, coarse, fine, fine],
        out_specs=[pl.BlockSpec((rb, length), lambda a: (a, 0))] * 3,
        out_shape=[out, out, out],
        compiler_params=_params(("parallel",)),
        name="dft_tables",
    )(c1[:, None, :], s1[:, None, :], c2, s2)
    alt = (1 - 2 * (jnp.arange(length, dtype=jnp.int32) % 2)).astype(F32)
    return cos_m, sin_f, sin_b, alt.reshape(length, 1)


def _filter_kernel(z_ref, w1_ref, b1_ref, f1_ref, w2_ref, b2_ref, f2_ref, w3f_ref, w3b_ref, b3f_ref, b3b_ref,
                   t_ref, dl_ref, alt_ref, hs_ref, hd_ref, nyq_ref, hid_scr):
    @pl.when((pl.program_id(0) == 0) & (pl.program_id(1) == 0))
    def _():
        h = jnp.sin(f1_ref[...] * (jnp.dot(z_ref[...], w1_ref[...], precision=HIGHEST,
                                           preferred_element_type=F32) + b1_ref[...]))
        h = jnp.sin(f2_ref[...] * (jnp.dot(h, w2_ref[...], precision=HIGHEST,
                                           preferred_element_type=F32) + b2_ref[...]))
        h_hi = h.astype(BF16)
        h_lo = (h - h_hi.astype(F32)).astype(BF16)
        nh = h.shape[1]
        for i, part in enumerate((h_hi, h_lo, h_hi, jnp.zeros_like(h_hi))):
            hid_scr[:, i * nh:(i + 1) * nh] = part

    def last_layer(w_ref, b_ref):
        w = w_ref[...]
        w_hi = w.astype(BF16)
        w_lo = (w - w_hi.astype(F32)).astype(BF16)
        rhs = jnp.concatenate([w_hi, w_hi, w_lo, jnp.zeros_like(w_hi)], axis=0)
        return jnp.dot(hid_scr[...], rhs, preferred_element_type=F32) + b_ref[...]

    decay = jnp.exp(-t_ref[...] * dl_ref[...])
    h_fwd = last_layer(w3f_ref, b3f_ref) * decay
    h_bwd = last_layer(w3b_ref, b3b_ref) * decay
    t_row = lax.broadcasted_iota(jnp.int32, h_fwd.shape, 0)
    h_bwd = jnp.where(t_row == 0, 0.0, h_bwd)
    h_sum = h_fwd + h_bwd
    hs_ref[...] = h_sum.astype(BF16)
    hd_ref[...] = (h_fwd - h_bwd).astype(BF16)
    nyq_ref[...] = jnp.sum(alt_ref[...] * h_sum, axis=0, keepdims=True)


def _filter_dft_kernel(hs_ref, hd_ref, nyq_ref, cos_ref, sin_ref, gr_ref, gi_ref):
    length = hs_ref.shape[0]
    g_re = jnp.dot(cos_ref[...], hs_ref[...], preferred_element_type=F32)
    g_im = jnp.dot(sin_ref[...], hd_ref[...], preferred_element_type=F32)
    row = lax.broadcasted_iota(jnp.int32, g_re.shape, 0) + pl.program_id(0) * g_re.shape[0]
    g_im = jnp.where(row == 0, nyq_ref[...], g_im)
    norm = jnp.where(row == 0, 0.5 / length, 1.0 / length)
    gr_ref[...] = g_re * norm
    gi_ref[...] = g_im * norm


def _filter_spectra(length, dft, fw1, fb1, ff1, fw2, fb2, ff2, fw3, fb3):
    cos_m, sin_f, _, alt = dft
    t = jnp.linspace(0.0, 1.0, length, dtype=F32)[:, None]
    w = 2.0 * math.pi * jnp.arange(length, dtype=F32)[:, None] / length
    bands = jnp.linspace(1e-4, FILTER_BANDS - 1, FILTER_BANDS, dtype=F32)[None, :]
    pad = FILTER_HIDDEN - FILTER_EMB
    z = jnp.concatenate([t, jnp.cos(bands * w), -jnp.sin(bands * w), jnp.zeros((length, pad), F32)], axis=-1)
    w1 = jnp.concatenate([fw1, jnp.zeros((pad, FILTER_HIDDEN), F32)], axis=0)
    deltas = jnp.abs(jnp.linspace(MIN_DECAY, MAX_DECAY, HYENA_WIDTH, dtype=F32))[None, :]
    tc = 256
    ncb = HYENA_WIDTH // tc
    hid = FILTER_HIDDEN
    row = lambda a: a.reshape(1, -1)
    const = lambda shape: pl.BlockSpec(shape, lambda o, j: (0, 0))
    taps = jax.ShapeDtypeStruct((HYENA_ORDER, length, HYENA_WIDTH), BF16)
    h_sum, h_dif, nyq = pl.pallas_call(
        _filter_kernel,
        grid=(HYENA_ORDER, ncb),
        in_specs=[
            const((length, hid)), const((hid, hid)), const((1, hid)), const((1, hid)),
            const((hid, hid)), const((1, hid)), const((1, hid)),
            pl.BlockSpec((hid, tc), lambda o, j: (0, (2 * o) * ncb + j)),
            pl.BlockSpec((hid, tc), lambda o, j: (0, (2 * o + 1) * ncb + j)),
            pl.BlockSpec((1, tc), lambda o, j: (0, (2 * o) * ncb + j)),
            pl.BlockSpec((1, tc), lambda o, j: (0, (2 * o + 1) * ncb + j)),
            const((length, 1)),
            pl.BlockSpec((1, tc), lambda o, j: (0, j)),
            const((length, 1)),
        ],
        out_specs=[pl.BlockSpec((None, length, tc), lambda o, j: (o, 0, j))] * 2
        + [pl.BlockSpec((None, 1, tc), lambda o, j: (o, 0, j))],
        out_shape=[taps, taps, jax.ShapeDtypeStruct((HYENA_ORDER, 1, HYENA_WIDTH), F32)],
        scratch_shapes=[pltpu.VMEM((length, 4 * hid), BF16)],
        compiler_params=_params(("arbitrary", "arbitrary")),
        name="filter_taps",
    )(z, w1, row(fb1), row(ff1), fw2, row(fb2), row(ff2), fw3, fw3, row(fb3), row(fb3), t, deltas, alt)

    tf, tg = _dft_tiles(length)
    ngb = HYENA_WIDTH // tg
    out = jax.ShapeDtypeStruct((HYENA_ORDER, length, HYENA_WIDTH), F32)
    return pl.pallas_call(
        _filter_dft_kernel,
        grid=(length // tf, HYENA_ORDER, ngb),
        in_specs=[
            pl.BlockSpec((None, length, tg), lambda f, o, j: (o, 0, j)),
            pl.BlockSpec((None, length, tg), lambda f, o, j: (o, 0, j)),
            pl.BlockSpec((None, 1, tg), lambda f, o, j: (o, 0, j)),
            pl.BlockSpec((tf, length), lambda f, o, j: (f, 0)),
            pl.BlockSpec((tf, length), lambda f, o, j: (f, 0)),
        ],
        out_specs=[pl.BlockSpec((None, tf, tg), lambda f, o, j: (o, f, j))] * 2,
        out_shape=[out, out],
        compiler_params=_params(("parallel", "parallel", "parallel")),
        name="filter_spectra",
    )(h_sum, h_dif, nyq, cos_m, sin_f)


def _dft_fwd_kernel(z_ref, cos_ref, sin_ref, gr_ref, gi_ref, yr_ref, yi_ref):
    z = z_ref[...]
    z_re = jnp.dot(cos_ref[...], z, preferred_element_type=F32)
    z_im = jnp.dot(sin_ref[...], z, preferred_element_type=F32)
    g_re = gr_ref[...]
    g_im = gi_ref[...]
    rr = z_re * g_re
    ii = z_im * g_im
    tf = z_re.shape[0]
    row = lax.broadcasted_iota(jnp.int32, z_re.shape, 0) + pl.program_id(0) * tf
    packed = row == 0
    yr_ref[...] = jnp.where(packed, rr, rr - ii).astype(BF16)
    yi_ref[...] = jnp.where(packed, ii, z_re * g_im + z_im * g_re).astype(BF16)


def _dft_tiles(length):
    return (1024, 512) if length > 512 else (length, HYENA_WIDTH)


def _dft_forward(z, order, dft, g_re, g_im):
    nb, length, _ = z.shape
    cos_m, sin_f = dft[0], dft[1]
    tf, tc = _dft_tiles(length)
    ncb = HYENA_WIDTH // tc
    out = jax.ShapeDtypeStruct((nb, length, HYENA_WIDTH), BF16)
    return pl.pallas_call(
        _dft_fwd_kernel,
        grid=(length // tf, nb, ncb),
        in_specs=[
            pl.BlockSpec((None, length, tc), lambda f, b, j: (b, 0, j)),
            pl.BlockSpec((tf, length), lambda f, b, j: (f, 0)),
            pl.BlockSpec((tf, length), lambda f, b, j: (f, 0)),
            pl.BlockSpec((None, tf, tc), lambda f, b, j: (order, f, j)),
            pl.BlockSpec((None, tf, tc), lambda f, b, j: (order, f, j)),
        ],
        out_specs=[pl.BlockSpec((None, tf, tc), lambda f, b, j: (b, f, j))] * 2,
        out_shape=[out, out],
        compiler_params=_params(("parallel", "parallel", "parallel")),
        name="dft_forward",
    )(z, cos_m, sin_f, g_re, g_im)


def _dft_inv_kernel(yr_ref, yi_ref, cos_ref, sin_ref, zin_ref, mul_ref, bias_ref, o_ref):
    y = (jnp.dot(cos_ref[...], yr_ref[...], preferred_element_type=F32)
         + jnp.dot(sin_ref[...], yi_ref[...], preferred_element_type=F32))
    y = y + bias_ref[...] * zin_ref[...].astype(F32)
    o_ref[...] = (mul_ref[...].astype(F32) * y).astype(o_ref.dtype)


def _dft_inverse(y_re, y_im, order, dft, z_in, mul, filt_bias):
    nb, length, _ = z_in.shape
    cos_m, sin_b = dft[0], dft[2]
    tt, tc = _dft_tiles(length)
    ncb = HYENA_WIDTH // tc
    return pl.pallas_call(
        _dft_inv_kernel,
        grid=(length // tt, nb, ncb),
        in_specs=[
            pl.BlockSpec((None, length, tc), lambda t, b, j: (b, 0, j)),
            pl.BlockSpec((None, length, tc), lambda t, b, j: (b, 0, j)),
            pl.BlockSpec((tt, length), lambda t, b, j: (t, 0)),
            pl.BlockSpec((tt, length), lambda t, b, j: (t, 0)),
            pl.BlockSpec((None, tt, tc), lambda t, b, j: (b, t, j)),
            pl.BlockSpec((None, tt, tc), lambda t, b, j: (b, t, j)),
            pl.BlockSpec((None, 1, tc), lambda t, b, j: (order, 0, j)),
        ],
        out_specs=pl.BlockSpec((None, tt, tc), lambda t, b, j: (b, t, j)),
        out_shape=jax.ShapeDtypeStruct((nb, length, HYENA_WIDTH), BF16),
        compiler_params=_params(("parallel", "parallel", "parallel")),
        name="dft_inverse",
    )(y_re, y_im, cos_m, sin_b, z_in, mul, filt_bias)


def _hyena_long_convs(v, x1, x2, dft, g_re, g_im, filt_bias):
    y_re, y_im = _dft_forward(v, 0, dft, g_re, g_im)
    z = _dft_inverse(y_re, y_im, 0, dft, v, x1, filt_bias)
    y_re, y_im = _dft_forward(z, 1, dft, g_re, g_im)
    return _dft_inverse(y_re, y_im, 1, dft, z, x2, filt_bias)


OUTPROJ_SUBTILES = 2


def _outproj_kernel(hy_ref, at_ref, x_ref, gate_ref, wa_ref, wb_ref, g_ref, b_ref, o_ref):
    sub = x_ref.shape[0] // OUTPROJ_SUBTILES
    for s in range(OUTPROJ_SUBTILES):
        rows = pl.ds(s * sub, sub)
        mix = (jnp.dot(hy_ref[rows, :], wa_ref[...], preferred_element_type=F32)
               + jnp.dot(at_ref[rows, :], wb_ref[...], preferred_element_type=F32))
        y = DEEPNORM_ALPHA * x_ref[rows, :] + gate_ref[...] * mix
        o_ref[rows, :] = _layer_norm(y, g_ref[...], b_ref[...])


def _out_projection(hy, attn, x, mod3, row0, w_out, ln_g, ln_b):
    nb, length, _ = x.shape
    tm = min(length, 512)
    half = D_MODEL // 2
    return pl.pallas_call(
        _outproj_kernel,
        grid=(nb, length // tm),
        in_specs=[
            pl.BlockSpec((None, tm, half), lambda b, i: (b, i, 0)),
            pl.BlockSpec((None, tm, half), lambda b, i: (b, i, 0)),
            pl.BlockSpec((None, tm, D_MODEL), lambda b, i: (b, i, 0)),
            _mod_spec(2, row0),
            pl.BlockSpec((half, D_MODEL), lambda b, i: (0, 0)),
            pl.BlockSpec((half, D_MODEL), lambda b, i: (1, 0)),
            pl.BlockSpec((1, D_MODEL), lambda b, i: (0, 0)),
            pl.BlockSpec((1, D_MODEL), lambda b, i: (0, 0)),
        ],
        out_specs=pl.BlockSpec((None, tm, D_MODEL), lambda b, i: (b, i, 0)),
        out_shape=jax.ShapeDtypeStruct((nb, length, D_MODEL), F32),
        compiler_params=_params(("parallel", "parallel")),
        name="out_projection",
    )(hy, attn, x, mod3, w_out, w_out, ln_g.reshape(1, D_MODEL), ln_b.reshape(1, D_MODEL))


def _mlp_kernel(x_ref, sh_ref, sc_ref, gate_ref, wu_ref, wd_ref, g_ref, b_ref, o_ref, h_scr, acc_scr):
    f = pl.program_id(2)

    @pl.when(f == 0)
    def _():
        h_scr[...] = (x_ref[...] * (1.0 + sc_ref[...]) + sh_ref[...]).astype(BF16)
        acc_scr[...] = jnp.zeros_like(acc_scr)

    a = jnp.maximum(jnp.dot(h_scr[...], wu_ref[...], preferred_element_type=F32), 0.0)
    acc_scr[...] += jnp.dot((a * a).astype(BF16), wd_ref[...], preferred_element_type=F32)

    @pl.when(f == pl.num_programs(2) - 1)
    def _():
        y = DEEPNORM_ALPHA * x_ref[...] + gate_ref[...] * acc_scr[...]
        o_ref[...] = _layer_norm(y, g_ref[...], b_ref[...])


def _mlp(x, mod3, row0, w_up, w_down, ln_g, ln_b):
    nb, length, _ = x.shape
    tm, tf = min(length, 512), 1024
    return pl.pallas_call(
        _mlp_kernel,
        grid=(nb, length // tm, D_FF // tf),
        in_specs=[
            pl.BlockSpec((None, tm, D_MODEL), lambda b, i, f: (b, i, 0)),
            _mod_spec(3, row0),
            _mod_spec(4, row0),
            _mod_spec(5, row0),
            pl.BlockSpec((D_MODEL, tf), lambda b, i, f: (0, f)),
            pl.BlockSpec((tf, D_MODEL), lambda b, i, f: (f, 0)),
            pl.BlockSpec((1, D_MODEL), lambda b, i, f: (0, 0)),
            pl.BlockSpec((1, D_MODEL), lambda b, i, f: (0, 0)),
        ],
        out_specs=pl.BlockSpec((None, tm, D_MODEL), lambda b, i, f: (b, i, 0)),
        out_shape=jax.ShapeDtypeStruct((nb, length, D_MODEL), F32),
        scratch_shapes=[pltpu.VMEM((tm, D_MODEL), BF16), pltpu.VMEM((tm, D_MODEL), F32)],
        compiler_params=_params(("parallel", "parallel", "arbitrary")),
        name="mlp",
    )(x, mod3, mod3, mod3, w_up, w_down, ln_g.reshape(1, D_MODEL), ln_b.reshape(1, D_MODEL))


def _rope_tables(length):
    rows = length // GRID_W
    row = jnp.broadcast_to(jnp.arange(rows, dtype=F32)[:, None], (rows, GRID_W)).reshape(-1)
    col = jnp.broadcast_to(jnp.arange(GRID_W, dtype=F32)[None, :], (rows, GRID_W)).reshape(-1)
    n = HEAD_DIM // 4
    inv = ROPE_THETA ** (-jnp.arange(n, dtype=F32) / n)
    ang = jnp.concatenate([row[:, None] * inv, col[:, None] * inv], axis=-1)
    cos, sin = jnp.cos(ang), jnp.sin(ang)
    return jnp.concatenate([cos, cos], axis=-1), jnp.concatenate([-sin, sin], axis=-1)


def _trunk_layer(x, mod3, row0, ctx_kv, rope_tables, emit_cache, weights, filt):
    (w_in, conv_w, conv_b, filt_bias, q_gain, k_gain, w_out, ln1_g, ln1_b, w_up, w_down, ln2_g, ln2_b) = weights
    dft, g_re, g_im = filt
    proj = _in_projection(x, mod3, row0, w_in)
    v, x1, x2 = _short_conv(proj, conv_w, conv_b)
    hy = _hyena_long_convs(v, x1, x2, dft, g_re, g_im, filt_bias)
    qkv = _qkv_prep(proj, q_gain, k_gain, rope_tables, emit_cache)
    q, k, vv = qkv[:3]
    if ctx_kv is not None:
        k = jnp.concatenate([ctx_kv[0], k], axis=1)
        vv = jnp.concatenate([ctx_kv[1], vv], axis=1)
    attn = _attention(q, k, vv)
    x = _out_projection(hy, attn, x, mod3, row0, w_out, ln1_g, ln1_b)
    x = _mlp(x, mod3, row0, w_up, w_down, ln2_g, ln2_b)
    return x, qkv[3:]


def kernel(x_prompt, x_sample, cache_k, cache_v, c, c_ctx, w_mod, b_mod, w_in, conv_w, conv_b, filt_w1, filt_b1, filt_freq1, filt_w2, filt_b2, filt_freq2, filt_w3, filt_b3, filt_bias, q_gain, k_gain, w_out, ln1_g, ln1_b, w_up, w_down, ln2_g, ln2_b):
    batch, seq, _ = x_prompt.shape
    dec_batch, dec_seq, _ = x_sample.shape
    past = cache_k.shape[2]
    assert w_mod.shape[0] == DEPTH == 1 and 1 + dec_batch <= MOD_ROWS

    cc = jnp.concatenate([c_ctx[None], c, jnp.zeros((MOD_ROWS - 1 - dec_batch, D_MODEL), F32)], axis=0)
    mod3 = _modulation(cc, w_mod[0], b_mod[0]).reshape(MOD_ROWS, 1, N_MOD * D_MODEL)

    weights = (w_in[0].astype(BF16), conv_w[0], conv_b[0].reshape(1, -1),
               filt_bias[0].reshape(HYENA_ORDER, 1, HYENA_WIDTH), q_gain[0], k_gain[0],
               w_out[0].astype(BF16), ln1_g[0], ln1_b[0], w_up[0].astype(BF16), w_down[0].astype(BF16),
               ln2_g[0], ln2_b[0])
    filt_w = (filt_w1[0], filt_b1[0], filt_freq1[0], filt_w2[0], filt_b2[0], filt_freq2[0], filt_w3[0], filt_b3[0])

    def filt(length):
        dft = _dft_matrices(length)
        g_re, g_im = _filter_spectra(length, dft, *filt_w)
        return dft, g_re, g_im

    per_seq = lambda a: a.reshape(batch, seq, a.shape[-1])
    flat = lambda a: a.reshape(1, batch * seq, a.shape[-1])
    (w_in_b, conv_w0, conv_b0, fbias, qg, kg, w_out_b, l1g, l1b, w_up_b, w_down_b, l2g, l2b) = weights
    dft_p, gre_p, gim_p = filt(seq)
    xp = flat(x_prompt)
    proj = _in_projection(xp, mod3, 0, w_in_b)
    v, x1, x2 = _short_conv(per_seq(proj), conv_w0, conv_b0)
    hy = _hyena_long_convs(v, x1, x2, dft_p, gre_p, gim_p, fbias)
    q, k, vv, k_cache, v_cache = _qkv_prep(proj, qg, kg, None, True)
    attn = _attention(per_seq(q), per_seq(k), per_seq(vv))
    xp = _out_projection(flat(hy), flat(attn), xp, mod3, 0, w_out_b, l1g, l1b)
    xp = _mlp(xp, mod3, 0, w_up_b, w_down_b, l2g, l2b)
    y_prompt = xp.reshape(batch, seq, D_MODEL)
    new_cache_k = k_cache.reshape(batch, 1, seq, N_KV_HEADS, HEAD_DIM)
    new_cache_v = v_cache.reshape(batch, 1, seq, N_KV_HEADS, HEAD_DIM)

    ctx_kv = (cache_k[:, 0].reshape(dec_batch, past, KV_DIM).astype(BF16),
              cache_v[:, 0].reshape(dec_batch, past, KV_DIM).astype(BF16))
    y_sample, _ = _trunk_layer(x_sample, mod3, 1, ctx_kv, _rope_tables(dec_seq), False, weights, filt(dec_seq))
    return (y_prompt, y_sample, new_cache_k, new_cache_v)
```

```python
import functools
import math

import jax
import jax.numpy as jnp
from jax import lax
from jax.experimental import pallas as pl
from jax.experimental.pallas import tpu as pltpu

F32 = jnp.float32
BF16 = jnp.bfloat16

D_MODEL = 2048
GRID_W = 64
HEAD_DIM = 128
HYENA_WIDTH = D_MODEL // 2
N_Q_HEADS = (D_MODEL - HYENA_WIDTH) // HEAD_DIM
N_KV_HEADS = 2
GQA_GROUP = N_Q_HEADS // N_KV_HEADS
Q_DIM = N_Q_HEADS * HEAD_DIM
KV_DIM = N_KV_HEADS * HEAD_DIM
QKV_DIM = Q_DIM + 2 * KV_DIM
IN_DIM = 3 * HYENA_WIDTH + QKV_DIM
D_FF = 4 * D_MODEL
HYENA_ORDER = 2
FILTER_EMB = 33
FILTER_BANDS = (FILTER_EMB - 1) // 2
FILTER_HIDDEN = 64
MIN_DECAY = math.log(1e-2) / 1.5
MAX_DECAY = math.log(1e-2) / 0.3
ROPE_THETA = 10000.0
LN_EPS = 1e-5
RMS_EPS = 1e-6
DEPTH = 1
DEEPNORM_ALPHA = (2 * DEPTH) ** 0.25
N_MOD = 6
MOD_ROWS = 16

V7X_VMEM_LIMIT = 56 * 1024 * 1024
HIGHEST = lax.Precision.HIGHEST


def _params(semantics):
    return pltpu.CompilerParams(dimension_semantics=semantics, vmem_limit_bytes=V7X_VMEM_LIMIT)


def _layer_norm(y, g, b):
    mu = jnp.mean(y, axis=-1, keepdims=True)
    yc = y - mu
    var = jnp.mean(yc * yc, axis=-1, keepdims=True)
    return yc * lax.rsqrt(var + LN_EPS) * g + b


def _mod_kernel(c_ref, w_ref, b_ref, o_ref):
    c = c_ref[...]
    s = c / (1.0 + jnp.exp(-c))
    s_hi = s.astype(BF16)
    s_lo = (s - s_hi.astype(F32)).astype(BF16)
    w = w_ref[...]
    w_hi = w.astype(BF16)
    w_lo = (w - w_hi.astype(F32)).astype(BF16)
    main = jnp.dot(jnp.concatenate([s_hi, s_lo], axis=0), w_hi, preferred_element_type=F32)
    o_ref[...] = (main[:MOD_ROWS] + main[MOD_ROWS:]
                  + jnp.dot(s_hi, w_lo, preferred_element_type=F32) + b_ref[...])


def _modulation(cc, w_mod, b_mod):
    tn = 1024
    n = N_MOD * D_MODEL
    return pl.pallas_call(
        _mod_kernel,
        grid=(n // tn,),
        in_specs=[
            pl.BlockSpec((MOD_ROWS, D_MODEL), lambda j: (0, 0)),
            pl.BlockSpec((D_MODEL, tn), lambda j: (0, j)),
            pl.BlockSpec((1, tn), lambda j: (0, j)),
        ],
        out_specs=pl.BlockSpec((MOD_ROWS, tn), lambda j: (0, j)),
        out_shape=jax.ShapeDtypeStruct((MOD_ROWS, n), F32),
        compiler_params=_params(("parallel",)),
        name="modulation",
    )(cc, w_mod, b_mod.reshape(1, n))


def _mod_spec(chunk, row0):
    return pl.BlockSpec((None, 1, D_MODEL), lambda b, *_: (b + row0, 0, chunk))


HALO = 16


def _hyena_inproj_kernel(x_ref, xp_ref, xn_ref, sh_ref, sc_ref, w_ref, cw_ref, cb_ref, u_ref, h_ref, h_scr,
                         *, seq_len):
    tm = x_ref.shape[0]
    rows = tm + 2 * HALO

    @pl.when(pl.program_id(2) == 0)
    def _():
        def modulate(x):
            return (x * (1.0 + sc_ref[...]) + sh_ref[...]).astype(BF16)

        h = modulate(x_ref[...])
        h_ref[...] = h
        h_scr[HALO:HALO + tm, :] = h
        h_scr[0:HALO, :] = modulate(xp_ref[...])
        h_scr[HALO + tm:rows, :] = modulate(xn_ref[...])

    p = jnp.dot(h_scr[...], w_ref[...], preferred_element_type=F32)
    r = lax.broadcasted_iota(jnp.int32, (rows, 1), 0)
    t = (pl.program_id(1) * tm + r + (seq_len - HALO)) & (seq_len - 1)
    prev = jnp.where(t == 0, 0.0, pltpu.roll(p, 1, 0))
    nxt = jnp.where(t == seq_len - 1, 0.0, pltpu.roll(p, rows - 1, 0))
    u = prev * cw_ref[0:1, :] + p * cw_ref[1:2, :] + nxt * cw_ref[2:3, :] + cb_ref[...]
    u_ref[...] = u[HALO:HALO + tm].astype(u_ref.dtype)


def _hyena_in_projection(x, mod3, row0, w_in, conv_w, conv_b, seq_len):
    nb, length, _ = x.shape
    assert seq_len & (seq_len - 1) == 0 and length % seq_len == 0
    tm, tn = min(length, 1024), HYENA_WIDTH
    nh = tm // HALO
    last = length // HALO - 1
    return pl.pallas_call(
        functools.partial(_hyena_inproj_kernel, seq_len=seq_len),
        grid=(nb, length // tm, 3 * HYENA_WIDTH // tn),
        in_specs=[
            pl.BlockSpec((None, tm, D_MODEL), lambda b, i, j: (b, i, 0)),
            pl.BlockSpec((None, HALO, D_MODEL), lambda b, i, j: (b, jnp.maximum(i * nh - 1, 0), 0)),
            pl.BlockSpec((None, HALO, D_MODEL), lambda b, i, j: (b, jnp.minimum((i + 1) * nh, last), 0)),
            _mod_spec(0, row0),
            _mod_spec(1, row0),
            pl.BlockSpec((D_MODEL, tn), lambda b, i, j: (0, j)),
            pl.BlockSpec((3, tn), lambda b, i, j: (0, j)),
            pl.BlockSpec((1, tn), lambda b, i, j: (0, j)),
        ],
        out_specs=[
            pl.BlockSpec((None, tm, tn), lambda b, i, j: (b, i, j)),
            pl.BlockSpec((None, tm, D_MODEL), lambda b, i, j: (b, i, 0)),
        ],
        out_shape=[
            jax.ShapeDtypeStruct((nb, length, 3 * HYENA_WIDTH), BF16),
            jax.ShapeDtypeStruct((nb, length, D_MODEL), BF16),
        ],
        scratch_shapes=[pltpu.VMEM((tm + 2 * HALO, D_MODEL), BF16)],
        compiler_params=_params(("parallel", "parallel", "arbitrary")),
        name="hyena_in_projection",
    )(x, x, x, mod3, mod3, w_in, conv_w, conv_b)


def _qkv_kernel(*refs, rope, emit_cache):
    h_ref, w_ref, qg_ref, kg_ref = refs[:4]
    refs = refs[4:]
    if rope:
        cc_ref, ss_ref = refs[:2]
        refs = refs[2:]
    q_ref, k_ref, v_ref = refs[:3]
    if emit_cache:
        kc_ref, vc_ref = refs[3:]
    scale = HEAD_DIM ** -0.5
    p = jnp.dot(h_ref[...], w_ref[...], preferred_element_type=F32)
    for h in range(N_Q_HEADS + N_KV_HEADS):
        x = p[:, h * HEAD_DIM:(h + 1) * HEAD_DIM]
        is_q = h < N_Q_HEADS
        gain = qg_ref[...] if is_q else kg_ref[...]
        y = x * lax.rsqrt(jnp.mean(x * x, axis=-1, keepdims=True) + RMS_EPS) * gain
        hk = h - N_Q_HEADS
        if emit_cache and not is_q:
            kc_ref[:, hk * HEAD_DIM:(hk + 1) * HEAD_DIM] = y
        if rope:
            y = y * cc_ref[...] + pltpu.roll(y, HEAD_DIM // 2, 1) * ss_ref[...]
        if is_q:
            q_ref[:, h * HEAD_DIM:(h + 1) * HEAD_DIM] = (y * scale).astype(BF16)
        else:
            k_ref[:, hk * HEAD_DIM:(hk + 1) * HEAD_DIM] = y.astype(BF16)
    v = p[:, Q_DIM + KV_DIM:]
    v_ref[...] = v.astype(BF16)
    if emit_cache:
        vc_ref[...] = v


def _qkv_projection(h, w_in, q_gain, k_gain, rope_tables, emit_cache):
    nb, length, _ = h.shape
    tl = min(length, 1024)
    rope = rope_tables is not None
    in_specs = [
        pl.BlockSpec((None, tl, D_MODEL), lambda b, i: (b, i, 0)),
        pl.BlockSpec((D_MODEL, QKV_DIM), lambda b, i: (0, 3 * HYENA_WIDTH // QKV_DIM)),
        pl.BlockSpec((1, HEAD_DIM), lambda b, i: (0, 0)),
        pl.BlockSpec((1, HEAD_DIM), lambda b, i: (0, 0)),
    ]
    args = [h, w_in, q_gain.reshape(1, HEAD_DIM), k_gain.reshape(1, HEAD_DIM)]
    if rope:
        in_specs += [pl.BlockSpec((tl, HEAD_DIM), lambda b, i: (i, 0))] * 2
        args += list(rope_tables)
    widths = [(Q_DIM, BF16), (KV_DIM, BF16), (KV_DIM, BF16)]
    if emit_cache:
        widths += [(KV_DIM, F32), (KV_DIM, F32)]
    return pl.pallas_call(
        functools.partial(_qkv_kernel, rope=rope, emit_cache=emit_cache),
        grid=(nb, length // tl),
        in_specs=in_specs,
        out_specs=[pl.BlockSpec((None, tl, w), lambda b, i: (b, i, 0)) for w, _ in widths],
        out_shape=[jax.ShapeDtypeStruct((nb, length, w), dt) for w, dt in widths],
        compiler_params=_params(("parallel", "parallel")),
        name="qkv_projection",
    )(*args)


def _attention_kernel(q_ref, k_ref, v_ref, o_ref):
    tq = q_ref.shape[0]
    q = jnp.concatenate([q_ref[:, g * HEAD_DIM:(g + 1) * HEAD_DIM] for g in range(GQA_GROUP)], axis=0)
    st = lax.dot_general(k_ref[...], q, (((1,), (1,)), ((), ())), preferred_element_type=F32)
    pt = jnp.exp(st - jnp.max(st, axis=0, keepdims=True))
    denom = jnp.sum(pt, axis=0, keepdims=True)
    ot = lax.dot_general(v_ref[...], pt.astype(BF16), (((0,), (0,)), ((), ())), preferred_element_type=F32)
    o = (ot / denom).T
    for g in range(GQA_GROUP):
        o_ref[:, g * HEAD_DIM:(g + 1) * HEAD_DIM] = o[g * tq:(g + 1) * tq].astype(o_ref.dtype)


def _attention(q, k, v):
    nb, lq, _ = q.shape
    lk = k.shape[1]
    tq = 256
    gw = GQA_GROUP * HEAD_DIM
    return pl.pallas_call(
        _attention_kernel,
        grid=(nb, N_KV_HEADS, lq // tq),
        in_specs=[
            pl.BlockSpec((None, tq, gw), lambda b, h, i: (b, i, h)),
            pl.BlockSpec((None, lk, HEAD_DIM), lambda b, h, i: (b, 0, h)),
            pl.BlockSpec((None, lk, HEAD_DIM), lambda b, h, i: (b, 0, h)),
        ],
        out_specs=pl.BlockSpec((None, tq, gw), lambda b, h, i: (b, i, h)),
        out_shape=jax.ShapeDtypeStruct((nb, lq, Q_DIM), BF16),
        compiler_params=_params(("parallel", "parallel", "parallel")),
        name="attention",
    )(q, k, v)


DFT_ROW_BLOCK = 64


def _dft_table_kernel(c1_ref, s1_ref, c2_ref, s2_ref, cos_ref, sinf_ref, sinb_ref):
    c1, s1, c2, s2 = c1_ref[...], s1_ref[...], c2_ref[...], s2_ref[...]
    cos = c1 * c2 - s1 * s2
    sin = -(s1 * c2 + c1 * s2)
    row = lax.broadcasted_iota(jnp.int32, cos.shape, 0) + pl.program_id(0) * cos.shape[0]
    col = lax.broadcasted_iota(jnp.int32, cos.shape, 1)
    cos_ref[...] = cos.astype(BF16)
    sinf_ref[...] = jnp.where(row == 0, (1 - 2 * (col & 1)).astype(F32), sin).astype(BF16)
    sinb_ref[...] = jnp.where(col == 0, (1 - 2 * (row & 1)).astype(F32), sin).astype(BF16)


def _dft_matrices(length):
    rb = DFT_ROW_BLOCK
    t = jnp.arange(length, dtype=jnp.int32)[None, :]

    def table(mult):
        ang = ((mult * t) % (2 * length)).astype(F32) * (math.pi / length)
        return jnp.cos(ang), jnp.sin(ang)

    c1, s1 = table(rb * jnp.arange(length // rb, dtype=jnp.int32)[:, None])
    c2, s2 = table(jnp.arange(rb, dtype=jnp.int32)[:, None])
    coarse = pl.BlockSpec((None, 1, length), lambda a: (a, 0, 0))
    fine = pl.BlockSpec((rb, length), lambda a: (0, 0))
    out = jax.ShapeDtypeStruct((length, length), BF16)
    cos_m, sin_f, sin_b = pl.pallas_call(
        _dft_table_kernel,
        grid=(length // rb,),
        in_specs=[coarse, coarse, fine, fine],
        out_specs=[pl.BlockSpec((rb, length), lambda a: (a, 0))] * 3,
        out_shape=[out, out, out],
        compiler_params=_params(("parallel",)),
        name="dft_tables",
    )(c1[:, None, :], s1[:, None, :], c2, s2)
    alt = (1 - 2 * (jnp.arange(length, dtype=jnp.int32) % 2)).astype(F32)
    return cos_m, sin_f, sin_b, alt.reshape(length, 1)


def _filter_kernel(z_ref, w1_ref, b1_ref, f1_ref, w2_ref, b2_ref, f2_ref, w3f_ref, w3b_ref, b3f_ref, b3b_ref,
                   t_ref, dl_ref, alt_ref, hs_ref, hd_ref, nyq_ref, hid_scr):
    @pl.when((pl.program_id(0) == 0) & (pl.program_id(1) == 0))
    def _():
        h = jnp.sin(f1_ref[...] * (jnp.dot(z_ref[...], w1_ref[...], precision=HIGHEST,
                                           preferred_element_type=F32) + b1_ref[...]))
        h = jnp.sin(f2_ref[...] * (jnp.dot(h, w2_ref[...], precision=HIGHEST,
                                           preferred_element_type=F32) + b2_ref[...]))
        h_hi = h.astype(BF16)
        h_lo = (h - h_hi.astype(F32)).astype(BF16)
        nh = h.shape[1]
        for i, part in enumerate((h_hi, h_lo, h_hi, jnp.zeros_like(h_hi))):
            hid_scr[:, i * nh:(i + 1) * nh] = part

    def last_layer(w_ref, b_ref):
        w = w_ref[...]
        w_hi = w.astype(BF16)
        w_lo = (w - w_hi.astype(F32)).astype(BF16)
        rhs = jnp.concatenate([w_hi, w_hi, w_lo, jnp.zeros_like(w_hi)], axis=0)
        return jnp.dot(hid_scr[...], rhs, preferred_element_type=F32) + b_ref[...]

    decay = jnp.exp(-t_ref[...] * dl_ref[...])
    h_fwd = last_layer(w3f_ref, b3f_ref) * decay
    h_bwd = last_layer(w3b_ref, b3b_ref) * decay
    t_row = lax.broadcasted_iota(jnp.int32, h_fwd.shape, 0)
    h_bwd = jnp.where(t_row == 0, 0.0, h_bwd)
    h_sum = h_fwd + h_bwd
    hs_ref[...] = h_sum.astype(BF16)
    hd_ref[...] = (h_fwd - h_bwd).astype(BF16)
    nyq_ref[...] = jnp.sum(alt_ref[...] * h_sum, axis=0, keepdims=True)


def _filter_dft_kernel(hs_ref, hd_ref, nyq_ref, cos_ref, sin_ref, gr_ref, gi_ref):
    length = hs_ref.shape[0]
    g_re = jnp.dot(cos_ref[...], hs_ref[...], preferred_element_type=F32)
    g_im = jnp.dot(sin_ref[...], hd_ref[...], preferred_element_type=F32)
    row = lax.broadcasted_iota(jnp.int32, g_re.shape, 0) + pl.program_id(0) * g_re.shape[0]
    g_im = jnp.where(row == 0, nyq_ref[...], g_im)
    norm = jnp.where(row == 0, 0.5 / length, 1.0 / length)
    gr_ref[...] = g_re * norm
    gi_ref[...] = g_im * norm


def _filter_spectra(length, dft, fw1, fb1, ff1, fw2, fb2, ff2, fw3, fb3):
    cos_m, sin_f, _, alt = dft
    t = jnp.linspace(0.0, 1.0, length, dtype=F32)[:, None]
    w = 2.0 * math.pi * jnp.arange(length, dtype=F32)[:, None] / length
    bands = jnp.linspace(1e-4, FILTER_BANDS - 1, FILTER_BANDS, dtype=F32)[None, :]
    pad = FILTER_HIDDEN - FILTER_EMB
    z = jnp.concatenate([t, jnp.cos(bands * w), -jnp.sin(bands * w), jnp.zeros((length, pad), F32)], axis=-1)
    w1 = jnp.concatenate([fw1, jnp.zeros((pad, FILTER_HIDDEN), F32)], axis=0)
    deltas = jnp.abs(jnp.linspace(MIN_DECAY, MAX_DECAY, HYENA_WIDTH, dtype=F32))[None, :]
    tc = 256
    ncb = HYENA_WIDTH // tc
    hid = FILTER_HIDDEN
    row = lambda a: a.reshape(1, -1)
    const = lambda shape: pl.BlockSpec(shape, lambda o, j: (0, 0))
    taps = jax.ShapeDtypeStruct((HYENA_ORDER, length, HYENA_WIDTH), BF16)
    h_sum, h_dif, nyq = pl.pallas_call(
        _filter_kernel,
        grid=(HYENA_ORDER, ncb),
        in_specs=[
            const((length, hid)), const((hid, hid)), const((1, hid)), const((1, hid)),
            const((hid, hid)), const((1, hid)), const((1, hid)),
            pl.BlockSpec((hid, tc), lambda o, j: (0, (2 * o) * ncb + j)),
            pl.BlockSpec((hid, tc), lambda o, j: (0, (2 * o + 1) * ncb + j)),
            pl.BlockSpec((1, tc), lambda o, j: (0, (2 * o) * ncb + j)),
            pl.BlockSpec((1, tc), lambda o, j: (0, (2 * o + 1) * ncb + j)),
            const((length, 1)),
            pl.BlockSpec((1, tc), lambda o, j: (0, j)),
            const((length, 1)),
        ],
        out_specs=[pl.BlockSpec((None, length, tc), lambda o, j: (o, 0, j))] * 2
        + [pl.BlockSpec((None, 1, tc), lambda o, j: (o, 0, j))],
        out_shape=[taps, taps, jax.ShapeDtypeStruct((HYENA_ORDER, 1, HYENA_WIDTH), F32)],
        scratch_shapes=[pltpu.VMEM((length, 4 * hid), BF16)],
        compiler_params=_params(("arbitrary", "arbitrary")),
        name="filter_taps",
    )(z, w1, row(fb1), row(ff1), fw2, row(fb2), row(ff2), fw3, fw3, row(fb3), row(fb3), t, deltas, alt)

    tf, tg = _dft_tiles(length)
    ngb = HYENA_WIDTH // tg
    out = jax.ShapeDtypeStruct((HYENA_ORDER, length, HYENA_WIDTH), F32)
    return pl.pallas_call(
        _filter_dft_kernel,
        grid=(length // tf, HYENA_ORDER, ngb),
        in_specs=[
            pl.BlockSpec((None, length, tg), lambda f, o, j: (o, 0, j)),
            pl.BlockSpec((None, length, tg), lambda f, o, j: (o, 0, j)),
            pl.BlockSpec((None, 1, tg), lambda f, o, j: (o, 0, j)),
            pl.BlockSpec((tf, length), lambda f, o, j: (f, 0)),
            pl.BlockSpec((tf, length), lambda f, o, j: (f, 0)),
        ],
        out_specs=[pl.BlockSpec((None, tf, tg), lambda f, o, j: (o, f, j))] * 2,
        out_shape=[out, out],
        compiler_params=_params(("parallel", "parallel", "parallel")),
        name="filter_spectra",
    )(h_sum, h_dif, nyq, cos_m, sin_f)


def _dft_fwd_kernel(z_ref, cos_ref, sin_ref, gr_ref, gi_ref, yr_ref, yi_ref):
    z = z_ref[...]
    z_re = jnp.dot(cos_ref[...], z, preferred_element_type=F32)
    z_im = jnp.dot(sin_ref[...], z, preferred_element_type=F32)
    g_re = gr_ref[...]
    g_im = gi_ref[...]
    rr = z_re * g_re
    ii = z_im * g_im
    tf = z_re.shape[0]
    row = lax.broadcasted_iota(jnp.int32, z_re.shape, 0) + pl.program_id(0) * tf
    packed = row == 0
    yr_ref[...] = jnp.where(packed, rr, rr - ii).astype(BF16)
    yi_ref[...] = jnp.where(packed, ii, z_re * g_im + z_im * g_re).astype(BF16)


def _dft_tiles(length):
    return (1024, 512) if length > 512 else (length, HYENA_WIDTH)


def _dft_forward(z, order, dft, g_re, g_im):
    z_arr, z_grp = z
    nb, length, _ = z_arr.shape
    cos_m, sin_f = dft[0], dft[1]
    tf, tc = _dft_tiles(length)
    ncb = HYENA_WIDTH // tc
    out = jax.ShapeDtypeStruct((nb, length, HYENA_WIDTH), BF16)
    return pl.pallas_call(
        _dft_fwd_kernel,
        grid=(length // tf, ncb, nb),
        in_specs=[
            pl.BlockSpec((None, length, tc), lambda f, j, b: (b, 0, z_grp * ncb + j)),
            pl.BlockSpec((tf, length), lambda f, j, b: (f, 0)),
            pl.BlockSpec((tf, length), lambda f, j, b: (f, 0)),
            pl.BlockSpec((None, tf, tc), lambda f, j, b: (order, f, j)),
            pl.BlockSpec((None, tf, tc), lambda f, j, b: (order, f, j)),
        ],
        out_specs=[pl.BlockSpec((None, tf, tc), lambda f, j, b: (b, f, j))] * 2,
        out_shape=[out, out],
        compiler_params=_params(("parallel", "parallel", "parallel")),
        name="dft_forward",
    )(z_arr, cos_m, sin_f, g_re, g_im)


def _dft_inv_kernel(yr_ref, yi_ref, cos_ref, sin_ref, zin_ref, mul_ref, bias_ref, o_ref):
    y = (jnp.dot(cos_ref[...], yr_ref[...], preferred_element_type=F32)
         + jnp.dot(sin_ref[...], yi_ref[...], preferred_element_type=F32))
    y = y + bias_ref[...] * zin_ref[...].astype(F32)
    o_ref[...] = (mul_ref[...].astype(F32) * y).astype(o_ref.dtype)


def _dft_inverse(y_re, y_im, order, dft, z_in, mul, filt_bias):
    (z_arr, z_grp), (m_arr, m_grp) = z_in, mul
    nb, length, _ = z_arr.shape
    cos_m, sin_b = dft[0], dft[2]
    tt, tc = _dft_tiles(length)
    ncb = HYENA_WIDTH // tc
    return pl.pallas_call(
        _dft_inv_kernel,
        grid=(length // tt, ncb, nb),
        in_specs=[
            pl.BlockSpec((None, length, tc), lambda t, j, b: (b, 0, j)),
            pl.BlockSpec((None, length, tc), lambda t, j, b: (b, 0, j)),
            pl.BlockSpec((tt, length), lambda t, j, b: (t, 0)),
            pl.BlockSpec((tt, length), lambda t, j, b: (t, 0)),
            pl.BlockSpec((None, tt, tc), lambda t, j, b: (b, t, z_grp * ncb + j)),
            pl.BlockSpec((None, tt, tc), lambda t, j, b: (b, t, m_grp * ncb + j)),
            pl.BlockSpec((None, 1, tc), lambda t, j, b: (order, 0, j)),
        ],
        out_specs=pl.BlockSpec((None, tt, tc), lambda t, j, b: (b, t, j)),
        out_shape=jax.ShapeDtypeStruct((nb, length, HYENA_WIDTH), BF16),
        compiler_params=_params(("parallel", "parallel", "parallel")),
        name="dft_inverse",
    )(y_re, y_im, cos_m, sin_b, z_arr, m_arr, filt_bias)


def _hyena_long_convs(u, dft, g_re, g_im, filt_bias):
    y_re, y_im = _dft_forward((u, 0), 0, dft, g_re, g_im)
    z = _dft_inverse(y_re, y_im, 0, dft, (u, 0), (u, 1), filt_bias)
    y_re, y_im = _dft_forward((z, 0), 1, dft, g_re, g_im)
    return _dft_inverse(y_re, y_im, 1, dft, (z, 0), (u, 2), filt_bias)


OUTPROJ_SUBTILES = 4


def _outproj_kernel(hy_ref, at_ref, x_ref, gate_ref, wa_ref, wb_ref, g_ref, b_ref, o_ref):
    sub = x_ref.shape[0] // OUTPROJ_SUBTILES
    for s in range(OUTPROJ_SUBTILES):
        rows = pl.ds(s * sub, sub)
        mix = (jnp.dot(hy_ref[rows, :], wa_ref[...], preferred_element_type=F32)
               + jnp.dot(at_ref[rows, :], wb_ref[...], preferred_element_type=F32))
        y = DEEPNORM_ALPHA * x_ref[rows, :] + gate_ref[...] * mix
        o_ref[rows, :] = _layer_norm(y, g_ref[...], b_ref[...])


def _out_projection(hy, attn, x, mod3, row0, w_out, ln_g, ln_b):
    nb, length, _ = x.shape
    tm = min(length, 512)
    half = D_MODEL // 2
    return pl.pallas_call(
        _outproj_kernel,
        grid=(nb, length // tm),
        in_specs=[
            pl.BlockSpec((None, tm, half), lambda b, i: (b, i, 0)),
            pl.BlockSpec((None, tm, half), lambda b, i: (b, i, 0)),
            pl.BlockSpec((None, tm, D_MODEL), lambda b, i: (b, i, 0)),
            _mod_spec(2, row0),
            pl.BlockSpec((half, D_MODEL), lambda b, i: (0, 0)),
            pl.BlockSpec((half, D_MODEL), lambda b, i: (1, 0)),
            pl.BlockSpec((1, D_MODEL), lambda b, i: (0, 0)),
            pl.BlockSpec((1, D_MODEL), lambda b, i: (0, 0)),
        ],
        out_specs=pl.BlockSpec((None, tm, D_MODEL), lambda b, i: (b, i, 0)),
        out_shape=jax.ShapeDtypeStruct((nb, length, D_MODEL), F32),
        compiler_params=_params(("parallel", "parallel")),
        name="out_projection",
    )(hy, attn, x, mod3, w_out, w_out, ln_g.reshape(1, D_MODEL), ln_b.reshape(1, D_MODEL))


def _mlp_kernel(x_ref, sh_ref, sc_ref, gate_ref, wu_ref, wd_ref, g_ref, b_ref, o_ref, h_scr, acc_scr):
    f = pl.program_id(2)

    @pl.when(f == 0)
    def _():
        h_scr[...] = (x_ref[...] * (1.0 + sc_ref[...]) + sh_ref[...]).astype(BF16)
        acc_scr[...] = jnp.zeros_like(acc_scr)

    a = jnp.maximum(jnp.dot(h_scr[...], wu_ref[...], preferred_element_type=F32), 0.0)
    acc_scr[...] += jnp.dot((a * a).astype(BF16), wd_ref[...], preferred_element_type=F32)

    @pl.when(f == pl.num_programs(2) - 1)
    def _():
        y = DEEPNORM_ALPHA * x_ref[...] + gate_ref[...] * acc_scr[...]
        o_ref[...] = _layer_norm(y, g_ref[...], b_ref[...])


def _mlp(x, mod3, row0, w_up, w_down, ln_g, ln_b):
    nb, length, _ = x.shape
    tm, tf = min(length, 512), 1024
    return pl.pallas_call(
        _mlp_kernel,
        grid=(nb, length // tm, D_FF // tf),
        in_specs=[
            pl.BlockSpec((None, tm, D_MODEL), lambda b, i, f: (b, i, 0)),
            _mod_spec(3, row0),
            _mod_spec(4, row0),
            _mod_spec(5, row0),
            pl.BlockSpec((D_MODEL, tf), lambda b, i, f: (0, f)),
            pl.BlockSpec((tf, D_MODEL), lambda b, i, f: (f, 0)),
            pl.BlockSpec((1, D_MODEL), lambda b, i, f: (0, 0)),
            pl.BlockSpec((1, D_MODEL), lambda b, i, f: (0, 0)),
        ],
        out_specs=pl.BlockSpec((None, tm, D_MODEL), lambda b, i, f: (b, i, 0)),
        out_shape=jax.ShapeDtypeStruct((nb, length, D_MODEL), F32),
        scratch_shapes=[pltpu.VMEM((tm, D_MODEL), BF16), pltpu.VMEM((tm, D_MODEL), F32)],
        compiler_params=_params(("parallel", "parallel", "arbitrary")),
        name="mlp",
    )(x, mod3, mod3, mod3, w_up, w_down, ln_g.reshape(1, D_MODEL), ln_b.reshape(1, D_MODEL))


def _rope_tables(length):
    rows = length // GRID_W
    row = jnp.broadcast_to(jnp.arange(rows, dtype=F32)[:, None], (rows, GRID_W)).reshape(-1)
    col = jnp.broadcast_to(jnp.arange(GRID_W, dtype=F32)[None, :], (rows, GRID_W)).reshape(-1)
    n = HEAD_DIM // 4
    inv = ROPE_THETA ** (-jnp.arange(n, dtype=F32) / n)
    ang = jnp.concatenate([row[:, None] * inv, col[:, None] * inv], axis=-1)
    cos, sin = jnp.cos(ang), jnp.sin(ang)
    return jnp.concatenate([cos, cos], axis=-1), jnp.concatenate([-sin, sin], axis=-1)


def _trunk_layer(x, seq_len, mod3, row0, ctx_kv, rope_tables, emit_cache, weights, filt):
    (w_in, conv_w, conv_b, filt_bias, q_gain, k_gain, w_out, ln1_g, ln1_b, w_up, w_down, ln2_g, ln2_b) = weights
    dft, g_re, g_im = filt
    groups, tokens, _ = x.shape
    per_seq = lambda a: a.reshape(groups * tokens // seq_len, seq_len, a.shape[-1])
    grouped = lambda a: a.reshape(groups, tokens, a.shape[-1])
    u, h = _hyena_in_projection(x, mod3, row0, w_in, conv_w, conv_b, seq_len)
    hy = _hyena_long_convs(per_seq(u), dft, g_re, g_im, filt_bias)
    qkv = _qkv_projection(h, w_in, q_gain, k_gain, rope_tables, emit_cache)
    q, k, vv = (per_seq(a) for a in qkv[:3])
    if ctx_kv is not None:
        k = jnp.concatenate([ctx_kv[0], k], axis=1)
        vv = jnp.concatenate([ctx_kv[1], vv], axis=1)
    attn = _attention(q, k, vv)
    x = _out_projection(grouped(hy), grouped(attn), x, mod3, row0, w_out, ln1_g, ln1_b)
    x = _mlp(x, mod3, row0, w_up, w_down, ln2_g, ln2_b)
    return x, qkv[3:]


def kernel(x_prompt, x_sample, cache_k, cache_v, c, c_ctx, w_mod, b_mod, w_in, conv_w, conv_b, filt_w1, filt_b1, filt_freq1, filt_w2, filt_b2, filt_freq2, filt_w3, filt_b3, filt_bias, q_gain, k_gain, w_out, ln1_g, ln1_b, w_up, w_down, ln2_g, ln2_b):
    batch, seq, _ = x_prompt.shape
    dec_batch, dec_seq, _ = x_sample.shape
    past = cache_k.shape[2]
    assert w_mod.shape[0] == DEPTH == 1 and 1 + dec_batch <= MOD_ROWS

    cc = jnp.concatenate([c_ctx[None], c, jnp.zeros((MOD_ROWS - 1 - dec_batch, D_MODEL), F32)], axis=0)
    mod3 = _modulation(cc, w_mod[0], b_mod[0]).reshape(MOD_ROWS, 1, N_MOD * D_MODEL)

    weights = (w_in[0].astype(BF16), conv_w[0], conv_b[0].reshape(1, -1),
               filt_bias[0].reshape(HYENA_ORDER, 1, HYENA_WIDTH), q_gain[0], k_gain[0],
               w_out[0].astype(BF16), ln1_g[0], ln1_b[0], w_up[0].astype(BF16), w_down[0].astype(BF16),
               ln2_g[0], ln2_b[0])
    filt_w = (filt_w1[0], filt_b1[0], filt_freq1[0], filt_w2[0], filt_b2[0], filt_freq2[0], filt_w3[0], filt_b3[0])

    def filt(length):
        dft = _dft_matrices(length)
        g_re, g_im = _filter_spectra(length, dft, *filt_w)
        return dft, g_re, g_im

    xp, (k_cache, v_cache) = _trunk_layer(x_prompt.reshape(1, batch * seq, D_MODEL), seq, mod3, 0, None, None, True,
                                          weights, filt(seq))
    y_prompt = xp.reshape(batch, seq, D_MODEL)
    new_cache_k = k_cache.reshape(batch, 1, seq, N_KV_HEADS, HEAD_DIM)
    new_cache_v = v_cache.reshape(batch, 1, seq, N_KV_HEADS, HEAD_DIM)

    ctx_kv = (cache_k[:, 0].reshape(dec_batch, past, KV_DIM).astype(BF16),
              cache_v[:, 0].reshape(dec_batch, past, KV_DIM).astype(BF16))
    y_sample, _ = _trunk_layer(x_sample, dec_seq, mod3, 1, ctx_kv, _rope_tables(dec_seq), False, weights,
                               filt(dec_seq))
    return (y_prompt, y_sample, new_cache_k, new_cache_v)
```

```python
import functools
import math

import jax
import jax.numpy as jnp
from jax import lax
from jax.experimental import pallas as pl
from jax.experimental.pallas import tpu as pltpu

F32 = jnp.float32
BF16 = jnp.bfloat16

D_MODEL = 2048
GRID_W = 64
HEAD_DIM = 128
HYENA_WIDTH = D_MODEL // 2
N_Q_HEADS = (D_MODEL - HYENA_WIDTH) // HEAD_DIM
N_KV_HEADS = 2
GQA_GROUP = N_Q_HEADS // N_KV_HEADS
Q_DIM = N_Q_HEADS * HEAD_DIM
KV_DIM = N_KV_HEADS * HEAD_DIM
QKV_DIM = Q_DIM + 2 * KV_DIM
IN_DIM = 3 * HYENA_WIDTH + QKV_DIM
D_FF = 4 * D_MODEL
HYENA_ORDER = 2
FILTER_EMB = 33
FILTER_BANDS = (FILTER_EMB - 1) // 2
FILTER_HIDDEN = 64
MIN_DECAY = math.log(1e-2) / 1.5
MAX_DECAY = math.log(1e-2) / 0.3
ROPE_THETA = 10000.0
LN_EPS = 1e-5
RMS_EPS = 1e-6
DEPTH = 1
DEEPNORM_ALPHA = (2 * DEPTH) ** 0.25
N_MOD = 6
MOD_ROWS = 16

V7X_VMEM_LIMIT = 56 * 1024 * 1024
HIGHEST = lax.Precision.HIGHEST


def _params(semantics):
    return pltpu.CompilerParams(dimension_semantics=semantics, vmem_limit_bytes=V7X_VMEM_LIMIT)


def _layer_norm(y, g, b):
    mu = jnp.mean(y, axis=-1, keepdims=True)
    yc = y - mu
    var = jnp.mean(yc * yc, axis=-1, keepdims=True)
    return yc * lax.rsqrt(var + LN_EPS) * g + b


def _mod_kernel(c_ref, w_ref, b_ref, o_ref):
    c = c_ref[...]
    s = c / (1.0 + jnp.exp(-c))
    s_hi = s.astype(BF16)
    s_lo = (s - s_hi.astype(F32)).astype(BF16)
    w = w_ref[...]
    w_hi = w.astype(BF16)
    w_lo = (w - w_hi.astype(F32)).astype(BF16)
    main = jnp.dot(jnp.concatenate([s_hi, s_lo], axis=0), w_hi, preferred_element_type=F32)
    o_ref[...] = (main[:MOD_ROWS] + main[MOD_ROWS:]
                  + jnp.dot(s_hi, w_lo, preferred_element_type=F32) + b_ref[...])


def _modulation(cc, w_mod, b_mod):
    tn = 1024
    n = N_MOD * D_MODEL
    return pl.pallas_call(
        _mod_kernel,
        grid=(n // tn,),
        in_specs=[
            pl.BlockSpec((MOD_ROWS, D_MODEL), lambda j: (0, 0)),
            pl.BlockSpec((D_MODEL, tn), lambda j: (0, j)),
            pl.BlockSpec((1, tn), lambda j: (0, j)),
        ],
        out_specs=pl.BlockSpec((MOD_ROWS, tn), lambda j: (0, j)),
        out_shape=jax.ShapeDtypeStruct((MOD_ROWS, n), F32),
        compiler_params=_params(("parallel",)),
        name="modulation",
    )(cc, w_mod, b_mod.reshape(1, n))


def _mod_spec(chunk, row0):
    return pl.BlockSpec((None, 1, D_MODEL), lambda b, *_: (b + row0, 0, chunk))


HALO = 16


def _hyena_inproj_kernel(x_ref, xp_ref, xn_ref, sh_ref, sc_ref, w_ref, cw_ref, cb_ref, u_ref, h_ref, h_scr,
                         *, seq_len):
    tm = x_ref.shape[0]
    rows = tm + 2 * HALO

    @pl.when(pl.program_id(2) == 0)
    def _():
        def modulate(x):
            return (x * (1.0 + sc_ref[...]) + sh_ref[...]).astype(BF16)

        h = modulate(x_ref[...])
        h_ref[...] = h
        h_scr[HALO:HALO + tm, :] = h
        h_scr[0:HALO, :] = modulate(xp_ref[...])
        h_scr[HALO + tm:rows, :] = modulate(xn_ref[...])

    p = jnp.dot(h_scr[...], w_ref[...], preferred_element_type=F32)
    r = lax.broadcasted_iota(jnp.int32, (rows, 1), 0)
    t = (pl.program_id(1) * tm + r + (seq_len - HALO)) & (seq_len - 1)
    prev = jnp.where(t == 0, 0.0, pltpu.roll(p, 1, 0))
    nxt = jnp.where(t == seq_len - 1, 0.0, pltpu.roll(p, rows - 1, 0))
    u = prev * cw_ref[0:1, :] + p * cw_ref[1:2, :] + nxt * cw_ref[2:3, :] + cb_ref[...]
    u_ref[...] = u[HALO:HALO + tm].astype(u_ref.dtype)


def _hyena_in_projection(x, mod3, row0, w_in, conv_w, conv_b, seq_len):
    nb, length, _ = x.shape
    assert seq_len & (seq_len - 1) == 0 and length % seq_len == 0
    tm, tn = min(length, 1024), HYENA_WIDTH
    nh = tm // HALO
    last = length // HALO - 1
    return pl.pallas_call(
        functools.partial(_hyena_inproj_kernel, seq_len=seq_len),
        grid=(nb, length // tm, 3 * HYENA_WIDTH // tn),
        in_specs=[
            pl.BlockSpec((None, tm, D_MODEL), lambda b, i, j: (b, i, 0)),
            pl.BlockSpec((None, HALO, D_MODEL), lambda b, i, j: (b, jnp.maximum(i * nh - 1, 0), 0)),
            pl.BlockSpec((None, HALO, D_MODEL), lambda b, i, j: (b, jnp.minimum((i + 1) * nh, last), 0)),
            _mod_spec(0, row0),
            _mod_spec(1, row0),
            pl.BlockSpec((D_MODEL, tn), lambda b, i, j: (0, j)),
            pl.BlockSpec((3, tn), lambda b, i, j: (0, j)),
            pl.BlockSpec((1, tn), lambda b, i, j: (0, j)),
        ],
        out_specs=[
            pl.BlockSpec((None, tm, tn), lambda b, i, j: (b, i, j)),
            pl.BlockSpec((None, tm, D_MODEL), lambda b, i, j: (b, i, 0)),
        ],
        out_shape=[
            jax.ShapeDtypeStruct((nb, length, 3 * HYENA_WIDTH), BF16),
            jax.ShapeDtypeStruct((nb, length, D_MODEL), BF16),
        ],
        scratch_shapes=[pltpu.VMEM((tm + 2 * HALO, D_MODEL), BF16)],
        compiler_params=_params(("parallel", "parallel", "arbitrary")),
        name="hyena_in_projection",
    )(x, x, x, mod3, mod3, w_in, conv_w, conv_b)


def _qkv_kernel(*refs, rope, emit_cache):
    h_ref, w_ref, qg_ref, kg_ref = refs[:4]
    refs = refs[4:]
    if rope:
        cc_ref, ss_ref = refs[:2]
        refs = refs[2:]
    q_ref, k_ref, v_ref = refs[:3]
    if emit_cache:
        kc_ref, vc_ref = refs[3:]
    scale = HEAD_DIM ** -0.5
    p = jnp.dot(h_ref[...], w_ref[...], preferred_element_type=F32)
    for h in range(N_Q_HEADS + N_KV_HEADS):
        x = p[:, h * HEAD_DIM:(h + 1) * HEAD_DIM]
        is_q = h < N_Q_HEADS
        gain = qg_ref[...] if is_q else kg_ref[...]
        y = x * lax.rsqrt(jnp.mean(x * x, axis=-1, keepdims=True) + RMS_EPS) * gain
        hk = h - N_Q_HEADS
        if emit_cache and not is_q:
            kc_ref[:, hk * HEAD_DIM:(hk + 1) * HEAD_DIM] = y
        if rope:
            y = y * cc_ref[...] + pltpu.roll(y, HEAD_DIM // 2, 1) * ss_ref[...]
        if is_q:
            q_ref[:, h * HEAD_DIM:(h + 1) * HEAD_DIM] = (y * scale).astype(BF16)
        else:
            k_ref[:, hk * HEAD_DIM:(hk + 1) * HEAD_DIM] = y.astype(BF16)
    v = p[:, Q_DIM + KV_DIM:]
    v_ref[...] = v.astype(BF16)
    if emit_cache:
        vc_ref[...] = v


def _qkv_projection(h, w_in, q_gain, k_gain, rope_tables, emit_cache):
    nb, length, _ = h.shape
    tl = min(length, 1024)
    rope = rope_tables is not None
    in_specs = [
        pl.BlockSpec((None, tl, D_MODEL), lambda b, i: (b, i, 0)),
        pl.BlockSpec((D_MODEL, QKV_DIM), lambda b, i: (0, 3 * HYENA_WIDTH // QKV_DIM)),
        pl.BlockSpec((1, HEAD_DIM), lambda b, i: (0, 0)),
        pl.BlockSpec((1, HEAD_DIM), lambda b, i: (0, 0)),
    ]
    args = [h, w_in, q_gain.reshape(1, HEAD_DIM), k_gain.reshape(1, HEAD_DIM)]
    if rope:
        in_specs += [pl.BlockSpec((tl, HEAD_DIM), lambda b, i: (i, 0))] * 2
        args += list(rope_tables)
    widths = [(Q_DIM, BF16), (KV_DIM, BF16), (KV_DIM, BF16)]
    if emit_cache:
        widths += [(KV_DIM, F32), (KV_DIM, F32)]
    return pl.pallas_call(
        functools.partial(_qkv_kernel, rope=rope, emit_cache=emit_cache),
        grid=(nb, length // tl),
        in_specs=in_specs,
        out_specs=[pl.BlockSpec((None, tl, w), lambda b, i: (b, i, 0)) for w, _ in widths],
        out_shape=[jax.ShapeDtypeStruct((nb, length, w), dt) for w, dt in widths],
        compiler_params=_params(("parallel", "parallel")),
        name="qkv_projection",
    )(*args)


def _attention_kernel(q_ref, k_ref, v_ref, o_ref):
    tq = q_ref.shape[0]
    q = jnp.concatenate([q_ref[:, g * HEAD_DIM:(g + 1) * HEAD_DIM] for g in range(GQA_GROUP)], axis=0)
    st = lax.dot_general(k_ref[...], q, (((1,), (1,)), ((), ())), preferred_element_type=F32)
    pt = jnp.exp(st - jnp.max(st, axis=0, keepdims=True))
    denom = jnp.sum(pt, axis=0, keepdims=True)
    ot = lax.dot_general(v_ref[...], pt.astype(BF16), (((0,), (0,)), ((), ())), preferred_element_type=F32)
    o = (ot / denom).T
    for g in range(GQA_GROUP):
        o_ref[:, g * HEAD_DIM:(g + 1) * HEAD_DIM] = o[g * tq:(g + 1) * tq].astype(o_ref.dtype)


def _attention(q, k, v):
    nb, lq, _ = q.shape
    lk = k.shape[1]
    tq = 256
    gw = GQA_GROUP * HEAD_DIM
    return pl.pallas_call(
        _attention_kernel,
        grid=(nb, N_KV_HEADS, lq // tq),
        in_specs=[
            pl.BlockSpec((None, tq, gw), lambda b, h, i: (b, i, h)),
            pl.BlockSpec((None, lk, HEAD_DIM), lambda b, h, i: (b, 0, h)),
            pl.BlockSpec((None, lk, HEAD_DIM), lambda b, h, i: (b, 0, h)),
        ],
        out_specs=pl.BlockSpec((None, tq, gw), lambda b, h, i: (b, i, h)),
        out_shape=jax.ShapeDtypeStruct((nb, lq, Q_DIM), BF16),
        compiler_params=_params(("parallel", "parallel", "parallel")),
        name="attention",
    )(q, k, v)


DFT_ROW_BLOCK = 64


def _dft_table_kernel(c1_ref, s1_ref, c2_ref, s2_ref, cos_ref, sinf_ref, sinb_ref):
    c1, s1, c2, s2 = c1_ref[...], s1_ref[...], c2_ref[...], s2_ref[...]
    cos = c1 * c2 - s1 * s2
    sin = -(s1 * c2 + c1 * s2)
    row = lax.broadcasted_iota(jnp.int32, cos.shape, 0) + pl.program_id(0) * cos.shape[0]
    col = lax.broadcasted_iota(jnp.int32, cos.shape, 1)
    cos_ref[...] = cos.astype(BF16)
    sinf_ref[...] = jnp.where(row == 0, (1 - 2 * (col & 1)).astype(F32), sin).astype(BF16)
    sinb_ref[...] = jnp.where(col == 0, (1 - 2 * (row & 1)).astype(F32), sin).astype(BF16)


def _dft_matrices(length):
    rb = DFT_ROW_BLOCK
    t = jnp.arange(length, dtype=jnp.int32)[None, :]

    def table(mult):
        ang = ((mult * t) % (2 * length)).astype(F32) * (math.pi / length)
        return jnp.cos(ang), jnp.sin(ang)

    c1, s1 = table(rb * jnp.arange(length // rb, dtype=jnp.int32)[:, None])
    c2, s2 = table(jnp.arange(rb, dtype=jnp.int32)[:, None])
    coarse = pl.BlockSpec((None, 1, length), lambda a: (a, 0, 0))
    fine = pl.BlockSpec((rb, length), lambda a: (0, 0))
    out = jax.ShapeDtypeStruct((length, length), BF16)
    cos_m, sin_f, sin_b = pl.pallas_call(
        _dft_table_kernel,
        grid=(length // rb,),
        in_specs=[coarse, coarse, fine, fine],
        out_specs=[pl.BlockSpec((rb, length), lambda a: (a, 0))] * 3,
        out_shape=[out, out, out],
        compiler_params=_params(("parallel",)),
        name="dft_tables",
    )(c1[:, None, :], s1[:, None, :], c2, s2)
    alt = (1 - 2 * (jnp.arange(length, dtype=jnp.int32) % 2)).astype(F32)
    return cos_m, sin_f, sin_b, alt.reshape(length, 1)


def _filter_kernel(z_ref, w1_ref, b1_ref, f1_ref, w2_ref, b2_ref, f2_ref, w3f_ref, w3b_ref, b3f_ref, b3b_ref,
                   t_ref, dl_ref, alt_ref, hs_ref, hd_ref, nyq_ref, hid_scr):
    @pl.when((pl.program_id(0) == 0) & (pl.program_id(1) == 0))
    def _():
        h = jnp.sin(f1_ref[...] * (jnp.dot(z_ref[...], w1_ref[...], precision=HIGHEST,
                                           preferred_element_type=F32) + b1_ref[...]))
        h = jnp.sin(f2_ref[...] * (jnp.dot(h, w2_ref[...], precision=HIGHEST,
                                           preferred_element_type=F32) + b2_ref[...]))
        h_hi = h.astype(BF16)
        h_lo = (h - h_hi.astype(F32)).astype(BF16)
        nh = h.shape[1]
        for i, part in enumerate((h_hi, h_lo, h_hi, jnp.zeros_like(h_hi))):
            hid_scr[:, i * nh:(i + 1) * nh] = part

    def last_layer(w_ref, b_ref):
        w = w_ref[...]
        w_hi = w.astype(BF16)
        w_lo = (w - w_hi.astype(F32)).astype(BF16)
        rhs = jnp.concatenate([w_hi, w_hi, w_lo, jnp.zeros_like(w_hi)], axis=0)
        return jnp.dot(hid_scr[...], rhs, preferred_element_type=F32) + b_ref[...]

    decay = jnp.exp(-t_ref[...] * dl_ref[...])
    h_fwd = last_layer(w3f_ref, b3f_ref) * decay
    h_bwd = last_layer(w3b_ref, b3b_ref) * decay
    t_row = lax.broadcasted_iota(jnp.int32, h_fwd.shape, 0)
    h_bwd = jnp.where(t_row == 0, 0.0, h_bwd)
    h_sum = h_fwd + h_bwd
    hs_ref[...] = h_sum.astype(BF16)
    hd_ref[...] = (h_fwd - h_bwd).astype(BF16)
    nyq_ref[...] = jnp.sum(alt_ref[...] * h_sum, axis=0, keepdims=True)


def _filter_dft_kernel(hs_ref, hd_ref, nyq_ref, cos_ref, sin_ref, gr_ref, gi_ref):
    length = hs_ref.shape[0]
    g_re = jnp.dot(cos_ref[...], hs_ref[...], preferred_element_type=F32)
    g_im = jnp.dot(sin_ref[...], hd_ref[...], preferred_element_type=F32)
    row = lax.broadcasted_iota(jnp.int32, g_re.shape, 0) + pl.program_id(0) * g_re.shape[0]
    g_im = jnp.where(row == 0, nyq_ref[...], g_im)
    norm = jnp.where(row == 0, 0.5 / length, 1.0 / length)
    gr_ref[...] = g_re * norm
    gi_ref[...] = g_im * norm


def _filter_spectra(length, dft, fw1, fb1, ff1, fw2, fb2, ff2, fw3, fb3):
    cos_m, sin_f, _, alt = dft
    t = jnp.linspace(0.0, 1.0, length, dtype=F32)[:, None]
    w = 2.0 * math.pi * jnp.arange(length, dtype=F32)[:, None] / length
    bands = jnp.linspace(1e-4, FILTER_BANDS - 1, FILTER_BANDS, dtype=F32)[None, :]
    pad = FILTER_HIDDEN - FILTER_EMB
    z = jnp.concatenate([t, jnp.cos(bands * w), -jnp.sin(bands * w), jnp.zeros((length, pad), F32)], axis=-1)
    w1 = jnp.concatenate([fw1, jnp.zeros((pad, FILTER_HIDDEN), F32)], axis=0)
    deltas = jnp.abs(jnp.linspace(MIN_DECAY, MAX_DECAY, HYENA_WIDTH, dtype=F32))[None, :]
    tc = 256
    ncb = HYENA_WIDTH // tc
    hid = FILTER_HIDDEN
    row = lambda a: a.reshape(1, -1)
    const = lambda shape: pl.BlockSpec(shape, lambda o, j: (0, 0))
    taps = jax.ShapeDtypeStruct((HYENA_ORDER, length, HYENA_WIDTH), BF16)
    h_sum, h_dif, nyq = pl.pallas_call(
        _filter_kernel,
        grid=(HYENA_ORDER, ncb),
        in_specs=[
            const((length, hid)), const((hid, hid)), const((1, hid)), const((1, hid)),
            const((hid, hid)), const((1, hid)), const((1, hid)),
            pl.BlockSpec((hid, tc), lambda o, j: (0, (2 * o) * ncb + j)),
            pl.BlockSpec((hid, tc), lambda o, j: (0, (2 * o + 1) * ncb + j)),
            pl.BlockSpec((1, tc), lambda o, j: (0, (2 * o) * ncb + j)),
            pl.BlockSpec((1, tc), lambda o, j: (0, (2 * o + 1) * ncb + j)),
            const((length, 1)),
            pl.BlockSpec((1, tc), lambda o, j: (0, j)),
            const((length, 1)),
        ],
        out_specs=[pl.BlockSpec((None, length, tc), lambda o, j: (o, 0, j))] * 2
        + [pl.BlockSpec((None, 1, tc), lambda o, j: (o, 0, j))],
        out_shape=[taps, taps, jax.ShapeDtypeStruct((HYENA_ORDER, 1, HYENA_WIDTH), F32)],
        scratch_shapes=[pltpu.VMEM((length, 4 * hid), BF16)],
        compiler_params=_params(("arbitrary", "arbitrary")),
        name="filter_taps",
    )(z, w1, row(fb1), row(ff1), fw2, row(fb2), row(ff2), fw3, fw3, row(fb3), row(fb3), t, deltas, alt)

    tf, tg = _dft_tiles(length)
    ngb = HYENA_WIDTH // tg
    out = jax.ShapeDtypeStruct((HYENA_ORDER, length, HYENA_WIDTH), F32)
    return pl.pallas_call(
        _filter_dft_kernel,
        grid=(length // tf, HYENA_ORDER, ngb),
        in_specs=[
            pl.BlockSpec((None, length, tg), lambda f, o, j: (o, 0, j)),
            pl.BlockSpec((None, length, tg), lambda f, o, j: (o, 0, j)),
            pl.BlockSpec((None, 1, tg), lambda f, o, j: (o, 0, j)),
            pl.BlockSpec((tf, length), lambda f, o, j: (f, 0)),
            pl.BlockSpec((tf, length), lambda f, o, j: (f, 0)),
        ],
        out_specs=[pl.BlockSpec((None, tf, tg), lambda f, o, j: (o, f, j))] * 2,
        out_shape=[out, out],
        compiler_params=_params(("parallel", "parallel", "parallel")),
        name="filter_spectra",
    )(h_sum, h_dif, nyq, cos_m, sin_f)


def _dft_fwd_kernel(z_ref, cos_ref, sin_ref, gr_ref, gi_ref, yr_ref, yi_ref):
    z = z_ref[...]
    z_re = jnp.dot(cos_ref[...], z, preferred_element_type=F32)
    z_im = jnp.dot(sin_ref[...], z, preferred_element_type=F32)
    g_re = gr_ref[...]
    g_im = gi_ref[...]
    rr = z_re * g_re
    ii = z_im * g_im
    tf = z_re.shape[0]
    row = lax.broadcasted_iota(jnp.int32, z_re.shape, 0) + pl.program_id(0) * tf
    packed = row == 0
    yr_ref[...] = jnp.where(packed, rr, rr - ii).astype(BF16)
    yi_ref[...] = jnp.where(packed, ii, z_re * g_im + z_im * g_re).astype(BF16)


def _dft_tiles(length):
    return (1024, 512) if length > 512 else (length, HYENA_WIDTH)


def _dft_forward(z, order, dft, g_re, g_im):
    z_arr, z_grp = z
    nb, length, _ = z_arr.shape
    cos_m, sin_f = dft[0], dft[1]
    tf, tc = _dft_tiles(length)
    ncb = HYENA_WIDTH // tc
    out = jax.ShapeDtypeStruct((nb, length, HYENA_WIDTH), BF16)
    return pl.pallas_call(
        _dft_fwd_kernel,
        grid=(length // tf, ncb, nb),
        in_specs=[
            pl.BlockSpec((None, length, tc), lambda f, j, b: (b, 0, z_grp * ncb + j)),
            pl.BlockSpec((tf, length), lambda f, j, b: (f, 0)),
            pl.BlockSpec((tf, length), lambda f, j, b: (f, 0)),
            pl.BlockSpec((None, tf, tc), lambda f, j, b: (order, f, j)),
            pl.BlockSpec((None, tf, tc), lambda f, j, b: (order, f, j)),
        ],
        out_specs=[pl.BlockSpec((None, tf, tc), lambda f, j, b: (b, f, j))] * 2,
        out_shape=[out, out],
        compiler_params=_params(("parallel", "parallel", "parallel")),
        name="dft_forward",
    )(z_arr, cos_m, sin_f, g_re, g_im)


def _dft_inv_kernel(yr_ref, yi_ref, cos_ref, sin_ref, zin_ref, mul_ref, bias_ref, o_ref):
    y = (jnp.dot(cos_ref[...], yr_ref[...], preferred_element_type=F32)
         + jnp.dot(sin_ref[...], yi_ref[...], preferred_element_type=F32))
    y = y + bias_ref[...] * zin_ref[...].astype(F32)
    o_ref[...] = (mul_ref[...].astype(F32) * y).astype(o_ref.dtype)


def _dft_inverse(y_re, y_im, order, dft, z_in, mul, filt_bias):
    (z_arr, z_grp), (m_arr, m_grp) = z_in, mul
    nb, length, _ = z_arr.shape
    cos_m, sin_b = dft[0], dft[2]
    tt, tc = _dft_tiles(length)
    ncb = HYENA_WIDTH // tc
    return pl.pallas_call(
        _dft_inv_kernel,
        grid=(length // tt, ncb, nb),
        in_specs=[
            pl.BlockSpec((None, length, tc), lambda t, j, b: (b, 0, j)),
            pl.BlockSpec((None, length, tc), lambda t, j, b: (b, 0, j)),
            pl.BlockSpec((tt, length), lambda t, j, b: (t, 0)),
            pl.BlockSpec((tt, length), lambda t, j, b: (t, 0)),
            pl.BlockSpec((None, tt, tc), lambda t, j, b: (b, t, z_grp * ncb + j)),
            pl.BlockSpec((None, tt, tc), lambda t, j, b: (b, t, m_grp * ncb + j)),
            pl.BlockSpec((None, 1, tc), lambda t, j, b: (order, 0, j)),
        ],
        out_specs=pl.BlockSpec((None, tt, tc), lambda t, j, b: (b, t, j)),
        out_shape=jax.ShapeDtypeStruct((nb, length, HYENA_WIDTH), BF16),
        compiler_params=_params(("parallel", "parallel", "parallel")),
        name="dft_inverse",
    )(y_re, y_im, cos_m, sin_b, z_arr, m_arr, filt_bias)


def _hyena_long_convs(u, dft, g_re, g_im, filt_bias):
    y_re, y_im = _dft_forward((u, 0), 0, dft, g_re, g_im)
    z = _dft_inverse(y_re, y_im, 0, dft, (u, 0), (u, 1), filt_bias)
    y_re, y_im = _dft_forward((z, 0), 1, dft, g_re, g_im)
    return _dft_inverse(y_re, y_im, 1, dft, (z, 0), (u, 2), filt_bias)


OUTPROJ_SUBTILES = 4


def _outproj_kernel(hy_ref, at_ref, x_ref, gate_ref, sh_ref, sc_ref, wa_ref, wb_ref, g_ref, b_ref, o_ref, h_ref):
    sub = x_ref.shape[0] // OUTPROJ_SUBTILES
    for s in range(OUTPROJ_SUBTILES):
        rows = pl.ds(s * sub, sub)
        mix = (jnp.dot(hy_ref[rows, :], wa_ref[...], preferred_element_type=F32)
               + jnp.dot(at_ref[rows, :], wb_ref[...], preferred_element_type=F32))
        y = DEEPNORM_ALPHA * x_ref[rows, :] + gate_ref[...] * mix
        x_mid = _layer_norm(y, g_ref[...], b_ref[...])
        o_ref[rows, :] = x_mid
        h_ref[rows, :] = (x_mid * (1.0 + sc_ref[...]) + sh_ref[...]).astype(BF16)


def _out_projection(hy, attn, x, mod3, row0, w_out, ln_g, ln_b):
    nb, length, _ = x.shape
    tm = min(length, 512)
    half = D_MODEL // 2
    tile = pl.BlockSpec((None, tm, D_MODEL), lambda b, i: (b, i, 0))
    return pl.pallas_call(
        _outproj_kernel,
        grid=(nb, length // tm),
        in_specs=[
            pl.BlockSpec((None, tm, half), lambda b, i: (b, i, 0)),
            pl.BlockSpec((None, tm, half), lambda b, i: (b, i, 0)),
            tile,
            _mod_spec(2, row0),
            _mod_spec(3, row0),
            _mod_spec(4, row0),
            pl.BlockSpec((half, D_MODEL), lambda b, i: (0, 0)),
            pl.BlockSpec((half, D_MODEL), lambda b, i: (1, 0)),
            pl.BlockSpec((1, D_MODEL), lambda b, i: (0, 0)),
            pl.BlockSpec((1, D_MODEL), lambda b, i: (0, 0)),
        ],
        out_specs=[tile, tile],
        out_shape=[jax.ShapeDtypeStruct((nb, length, D_MODEL), F32), jax.ShapeDtypeStruct((nb, length, D_MODEL), BF16)],
        compiler_params=_params(("parallel", "parallel")),
        name="out_projection",
    )(hy, attn, x, mod3, mod3, mod3, w_out, w_out, ln_g.reshape(1, D_MODEL), ln_b.reshape(1, D_MODEL))


def _mlp_kernel(h_ref, x_ref, gate_ref, wu_ref, wd_ref, g_ref, b_ref, o_ref, acc_scr, *, n_tiles):
    t, f = pl.program_id(0), pl.program_id(1)
    rows = h_ref.shape[0] // pl.num_programs(1)
    cur = lax.rem(t, 2)

    @pl.when((t == 0) & (f == 0))
    def _():
        acc_scr[...] = jnp.zeros_like(acc_scr)

    def finish_previous_rows():
        r = pl.ds(pl.multiple_of(f * rows, rows), rows)
        y = DEEPNORM_ALPHA * x_ref[r, :] + gate_ref[...] * acc_scr[1 - cur, r, :]
        o_ref[r, :] = _layer_norm(y, g_ref[...], b_ref[...])

    @pl.when(t < n_tiles)
    def _():
        finish_previous_rows()
        a = jnp.maximum(jnp.dot(h_ref[...], wu_ref[...], preferred_element_type=F32), 0.0)
        part = jnp.dot((a * a).astype(BF16), wd_ref[...], preferred_element_type=F32)
        acc_scr[cur] = jnp.where(f == 0, 0.0, acc_scr[cur]) + part

    @pl.when(t == n_tiles)
    def _():
        finish_previous_rows()


def _mlp(h, x, mod3, row0, w_up, w_down, ln_g, ln_b):
    groups, tokens, _ = x.shape
    tm, tf = min(tokens, 512), 1024
    per_group = tokens // tm
    n_tiles = groups * per_group
    nf = D_FF // tf
    prev = lambda t: jnp.maximum(t - 1, 0)
    chunk = lambda t, f: jnp.where(t == n_tiles, nf - 1, f)
    flat = lambda a: a.reshape(groups * tokens, D_MODEL)
    out = pl.pallas_call(
        functools.partial(_mlp_kernel, n_tiles=n_tiles),
        grid=(n_tiles + 1, nf),
        in_specs=[
            pl.BlockSpec((tm, D_MODEL), lambda t, f: (jnp.minimum(t, n_tiles - 1), 0)),
            pl.BlockSpec((tm, D_MODEL), lambda t, f: (prev(t), 0)),
            pl.BlockSpec((None, 1, D_MODEL), lambda t, f: (row0 + prev(t) // per_group, 0, 5)),
            pl.BlockSpec((D_MODEL, tf), lambda t, f: (0, chunk(t, f))),
            pl.BlockSpec((tf, D_MODEL), lambda t, f: (chunk(t, f), 0)),
            pl.BlockSpec((1, D_MODEL), lambda t, f: (0, 0)),
            pl.BlockSpec((1, D_MODEL), lambda t, f: (0, 0)),
        ],
        out_specs=pl.BlockSpec((tm, D_MODEL), lambda t, f: (prev(t), 0)),
        out_shape=jax.ShapeDtypeStruct((groups * tokens, D_MODEL), F32),
        scratch_shapes=[pltpu.VMEM((2, tm, D_MODEL), F32)],
        compiler_params=_params(("arbitrary", "arbitrary")),
        name="mlp",
    )(flat(h), flat(x), mod3, w_up, w_down, ln_g.reshape(1, D_MODEL), ln_b.reshape(1, D_MODEL))
    return out.reshape(groups, tokens, D_MODEL)


def _rope_tables(length):
    rows = length // GRID_W
    row = jnp.broadcast_to(jnp.arange(rows, dtype=F32)[:, None], (rows, GRID_W)).reshape(-1)
    col = jnp.broadcast_to(jnp.arange(GRID_W, dtype=F32)[None, :], (rows, GRID_W)).reshape(-1)
    n = HEAD_DIM // 4
    inv = ROPE_THETA ** (-jnp.arange(n, dtype=F32) / n)
    ang = jnp.concatenate([row[:, None] * inv, col[:, None] * inv], axis=-1)
    cos, sin = jnp.cos(ang), jnp.sin(ang)
    return jnp.concatenate([cos, cos], axis=-1), jnp.concatenate([-sin, sin], axis=-1)


def _trunk_layer(x, seq_len, mod3, row0, ctx_kv, rope_tables, emit_cache, weights, filt):
    (w_in, conv_w, conv_b, filt_bias, q_gain, k_gain, w_out, ln1_g, ln1_b, w_up, w_down, ln2_g, ln2_b) = weights
    dft, g_re, g_im = filt
    groups, tokens, _ = x.shape
    per_seq = lambda a: a.reshape(groups * tokens // seq_len, seq_len, a.shape[-1])
    grouped = lambda a: a.reshape(groups, tokens, a.shape[-1])
    u, h = _hyena_in_projection(x, mod3, row0, w_in, conv_w, conv_b, seq_len)
    hy = _hyena_long_convs(per_seq(u), dft, g_re, g_im, filt_bias)
    qkv = _qkv_projection(h, w_in, q_gain, k_gain, rope_tables, emit_cache)
    q, k, vv = (per_seq(a) for a in qkv[:3])
    if ctx_kv is not None:
        k = jnp.concatenate([ctx_kv[0], k], axis=1)
        vv = jnp.concatenate([ctx_kv[1], vv], axis=1)
    attn = _attention(q, k, vv)
    x, h = _out_projection(grouped(hy), grouped(attn), x, mod3, row0, w_out, ln1_g, ln1_b)
    x = _mlp(h, x, mod3, row0, w_up, w_down, ln2_g, ln2_b)
    return x, qkv[3:]


def kernel(x_prompt, x_sample, cache_k, cache_v, c, c_ctx, w_mod, b_mod, w_in, conv_w, conv_b, filt_w1, filt_b1, filt_freq1, filt_w2, filt_b2, filt_freq2, filt_w3, filt_b3, filt_bias, q_gain, k_gain, w_out, ln1_g, ln1_b, w_up, w_down, ln2_g, ln2_b):
    batch, seq, _ = x_prompt.shape
    dec_batch, dec_seq, _ = x_sample.shape
    past = cache_k.shape[2]
    assert w_mod.shape[0] == DEPTH == 1 and 1 + dec_batch <= MOD_ROWS

    cc = jnp.concatenate([c_ctx[None], c, jnp.zeros((MOD_ROWS - 1 - dec_batch, D_MODEL), F32)], axis=0)
    mod3 = _modulation(cc, w_mod[0], b_mod[0]).reshape(MOD_ROWS, 1, N_MOD * D_MODEL)

    weights = (w_in[0].astype(BF16), conv_w[0], conv_b[0].reshape(1, -1),
               filt_bias[0].reshape(HYENA_ORDER, 1, HYENA_WIDTH), q_gain[0], k_gain[0],
               w_out[0].astype(BF16), ln1_g[0], ln1_b[0], w_up[0].astype(BF16), w_down[0].astype(BF16),
               ln2_g[0], ln2_b[0])
    filt_w = (filt_w1[0], filt_b1[0], filt_freq1[0], filt_w2[0], filt_b2[0], filt_freq2[0], filt_w3[0], filt_b3[0])

    def filt(length):
        dft = _dft_matrices(length)
        g_re, g_im = _filter_spectra(length, dft, *filt_w)
        return dft, g_re, g_im

    xp, (k_cache, v_cache) = _trunk_layer(x_prompt.reshape(1, batch * seq, D_MODEL), seq, mod3, 0, None, None, True,
                                          weights, filt(seq))
    y_prompt = xp.reshape(batch, seq, D_MODEL)
    new_cache_k = k_cache.reshape(batch, 1, seq, N_KV_HEADS, HEAD_DIM)
    new_cache_v = v_cache.reshape(batch, 1, seq, N_KV_HEADS, HEAD_DIM)

    ctx_kv = (cache_k[:, 0].reshape(dec_batch, past, KV_DIM).astype(BF16),
              cache_v[:, 0].reshape(dec_batch, past, KV_DIM).astype(BF16))
    y_sample, _ = _trunk_layer(x_sample, dec_seq, mod3, 1, ctx_kv, _rope_tables(dec_seq), False, weights,
                               filt(dec_seq))
    return (y_prompt, y_sample, new_cache_k, new_cache_v)
```
